```python
import jax
import jax.numpy as jnp
from jax import lax
import numpy as np


D_MODEL = 1024
BATCH = 1
SEQ = 16384
DEPTH = 2

GRID_W = 64
CTX_LEN = 256
N_EVEN = (DEPTH + 1) // 2
N_ODD = DEPTH // 2
EPS = 1e-6
A_HEADS = 4
A_DK = 128
A_DV = 128
A_KW = A_HEADS * A_DK
A_WIDTH = A_HEADS * A_DV
CHUNK = 64
B_HEADS = 8
B_DH = 64
B_WIDTH = B_HEADS * B_DH
WIN_ROWS = 8
WIN_COLS = 16
Q_BLOCK_ROWS = 2
IN_SIZES = (A_KW, A_KW, A_KW, A_WIDTH, A_WIDTH, B_WIDTH, B_WIDTH, B_WIDTH)
IN_WIDTH = 3 * A_KW + 2 * A_WIDTH + 3 * B_WIDTH
POOL_WINDOWS = (2, 4, 8, 16)
C_GROUP = D_MODEL // 4
N_EXPERTS = 16
N_GROUPS = 4
EXPERTS_PER_GROUP = N_EXPERTS // N_GROUPS
TOP_K = 2
D_EXPERT = 512

kernel_name = 'hybrid_hgrn2_natten_pool_moe_dit'


def rms_norm(x, g):
    xf = x.astype(jnp.float32)
    y = xf * lax.rsqrt(jnp.mean(xf * xf, axis=-1, keepdims=True) + EPS)
    return (y * g.astype(jnp.float32)).astype(x.dtype)


def adaln(cond, w, b):
    return jnp.split(jax.nn.silu(cond) @ w + b, 6, axis=-1)


def modulate(u, shift, scale):
    return u * (1 + scale) + shift


def to_heads(t, h):
    b_, l_, hd = t.shape
    return t.reshape(b_, l_, h, hd // h).transpose(0, 2, 1, 3)


def hgrn_lower_bounds(lb_logits):
    p = jax.nn.softmax(lb_logits.astype(jnp.float32), axis=0)
    return jnp.cumsum(p, axis=0)[1:] - p[0]


def gated_scan(q, k, v, log_f, s0):
    b_, h_, l_, _ = q.shape
    n = l_ // CHUNK

    def to_chunks(t):
        return jnp.moveaxis(t.reshape(b_, h_, n, CHUNK, t.shape[-1]), 2, 0)

    causal = jnp.tril(jnp.ones((CHUNK, CHUNK), bool))[:, :, None]

    def step(s, inp):
        qi, ki, vi, fi = inp
        bc = jnp.cumsum(fi, axis=2)
        diff = bc[:, :, :, None, :] - bc[:, :, None, :, :]
        decay = jnp.exp(jnp.where(causal, diff, -jnp.inf))
        scores = jnp.einsum('bhtk,bhsk,bhtsk->bhts', qi, ki, decay)
        o = (jnp.einsum('bhts,bhsv->bhtv', scores, vi)
             + jnp.einsum('bhtk,bhkv->bhtv', qi * jnp.exp(bc), s))
        b_last = bc[:, :, -1:, :]
        s_new = (jnp.exp(b_last[:, :, 0, :, None]) * s
                 + jnp.einsum('bhsk,bhsv->bhkv', ki * jnp.exp(b_last - bc), vi))
        return s_new, o

    s_fin, oc = lax.scan(step, s0, (to_chunks(q), to_chunks(k), to_chunks(v), to_chunks(log_f)))
    return jnp.moveaxis(oc, 0, 2).reshape(b_, h_, l_, v.shape[-1]), s_fin


def context_final_state(k, v, log_f):
    bc = jnp.cumsum(log_f, axis=2)
    return jnp.einsum('bhsk,bhsv->bhkv', k * jnp.exp(bc[:, :, -1:, :] - bc), v)


def hgrn2_direction(q_c, i_c, f_c, q_l, i_l, f_l, lb_dir, reverse, ctx_out):
    def heads(t):
        t = to_heads(t, A_HEADS)
        return t[:, :, ::-1] if reverse else t

    def gates(f_raw):
        f = lb_dir + (1.0 - lb_dir) * jax.nn.sigmoid(f_raw.astype(jnp.float32))
        return heads(1.0 - f), heads(jnp.log(f))

    scale = A_DK ** -0.5
    k_c, lf_c = gates(f_c)
    v_c = heads(i_c.astype(jnp.float32))
    if ctx_out:
        s0 = jnp.zeros(k_c.shape[:2] + (A_DK, A_DV), jnp.float32)
        o_c, s_c = gated_scan(heads(q_c.astype(jnp.float32)) * scale, k_c, v_c, lf_c, s0)
    else:
        o_c, s_c = None, context_final_state(k_c, v_c, lf_c)
    k_l, lf_l = gates(f_l)
    o_l, _ = gated_scan(heads(q_l.astype(jnp.float32)) * scale, k_l, heads(i_l.astype(jnp.float32)), lf_l, s_c)
    if reverse:
        o_l = o_l[:, :, ::-1]
        o_c = None if o_c is None else o_c[:, :, ::-1]
    return o_c, o_l


def hgrn2_readout(o, g, norm_g):
    o = o * lax.rsqrt(jnp.mean(o * o, axis=-1, keepdims=True) + EPS)
    b_, h_, l_, v_ = o.shape
    o = o.transpose(0, 2, 1, 3).reshape(b_, l_, h_ * v_) * norm_g.astype(jnp.float32)
    return (o * jax.nn.silu(g.astype(jnp.float32))).astype(g.dtype)


def neighbourhood_attention(q, k, v, k_ctx, v_ctx, rpb):
    b_, n_, h_, dh = q.shape
    rows = n_ // GRID_W
    wr = min(WIN_ROWS, rows)
    wc = WIN_COLS
    nwin = wr * wc
    qb = Q_BLOCK_ROWS * GRID_W
    scale = dh ** -0.5
    cols = jnp.arange(GRID_W)
    key_cols = jnp.clip(cols - wc // 2, 0, GRID_W - wc)[:, None] + jnp.arange(wc)
    dc = key_cols - cols[:, None] + (WIN_COLS - 1)
    rpb = rpb.astype(jnp.float32)

    def block(blk):
        r = blk * Q_BLOCK_ROWS + jnp.arange(Q_BLOCK_ROWS)
        key_rows = jnp.clip(r - wr // 2, 0, rows - wr)[:, None] + jnp.arange(wr)
        dr = key_rows - r[:, None] + (WIN_ROWS - 1)
        idx = (key_rows[:, None, :, None] * GRID_W + key_cols[None, :, None, :]).reshape(qb, nwin)
        bias = rpb[:, dr[:, None, :, None], dc[None, :, None, :]].reshape(h_, qb, nwin)
        q_blk = lax.dynamic_slice_in_dim(q, blk * qb, qb, axis=1)
        k_win = jnp.take(k, idx, axis=1)
        v_win = jnp.take(v, idx, axis=1)
        s_win = jnp.einsum('bqhd,bqkhd->bhqk', q_blk, k_win).astype(jnp.float32) * scale + bias
        s_ctx = jnp.einsum('bqhd,bchd->bhqc', q_blk, k_ctx).astype(jnp.float32) * scale
        p = jax.nn.softmax(jnp.concatenate([s_win, s_ctx], axis=-1), axis=-1).astype(q.dtype)
        return (jnp.einsum('bhqk,bqkhd->bqhd', p[..., :nwin], v_win)
                + jnp.einsum('bhqc,bchd->bqhd', p[..., nwin:], v_ctx))

    out = lax.map(block, jnp.arange(rows // Q_BLOCK_ROWS))
    return jnp.moveaxis(out, 0, 1).reshape(b_, n_, h_ * dh)


def context_attention(q, k, v):
    s = jnp.einsum('bqhd,bkhd->bhqk', q, k).astype(jnp.float32) * (q.shape[-1] ** -0.5)
    p = jax.nn.softmax(s, axis=-1).astype(q.dtype)
    o = jnp.einsum('bhqk,bkhd->bqhd', p, v)
    return o.reshape(o.shape[0], o.shape[1], -1)


def even_mixer(u_c, u_l, w_in, w_out, lb, a_norm_g, rpb, ctx_out):
    offs = [int(o) for o in np.cumsum(IN_SIZES)[:-1]]
    qa_c, ff_c, fb_c, ia_c, ga_c, qb_c, kb_c, vb_c = jnp.split(u_c @ w_in, offs, axis=-1)
    qa_l, ff_l, fb_l, ia_l, ga_l, qb_l, kb_l, vb_l = jnp.split(u_l @ w_in, offs, axis=-1)
    o_cf, o_lf = hgrn2_direction(qa_c, ia_c, ff_c, qa_l, ia_l, ff_l, lb[0], False, ctx_out)
    o_cb, o_lb = hgrn2_direction(qa_c, ia_c, fb_c, qa_l, ia_l, fb_l, lb[1], True, ctx_out)
    a_l = hgrn2_readout(o_lf + o_lb, ga_l, a_norm_g)
    bh = lambda t: t.reshape(t.shape[0], t.shape[1], B_HEADS, B_DH)
    b_l = neighbourhood_attention(bh(qb_l), bh(kb_l), bh(vb_l), bh(kb_c), bh(vb_c), rpb)
    y_l = jnp.concatenate([a_l, b_l.astype(a_l.dtype)], axis=-1) @ w_out
    if not ctx_out:
        return None, y_l
    a_c = hgrn2_readout(o_cf + o_cb, ga_c, a_norm_g)
    b_c = context_attention(bh(qb_c), bh(kb_c), bh(vb_c))
    y_c = jnp.concatenate([a_c, b_c.astype(a_c.dtype)], axis=-1) @ w_out
    return y_c, y_l


def pool_mixer(u, w_pool, pool_scale):
    b_, l_, d_ = u.shape
    uf = u.astype(jnp.float32)
    csum = jnp.concatenate([jnp.zeros((b_, 1, d_), jnp.float32), jnp.cumsum(uf, axis=1)], axis=1)
    t = np.arange(l_)
    outs = []
    for g, win in enumerate(POOL_WINDOWS):
        ch = slice(g * C_GROUP, (g + 1) * C_GROUP)
        lo = np.clip(t - win // 2, 0, l_)
        hi = np.clip(t + win // 2, 0, l_)
        count = (hi - lo).astype(np.float32)[:, None]
        cg = csum[:, :, ch]
        mean = (cg[:, hi] - cg[:, lo]) / count
        outs.append(jnp.einsum('bld,de->ble', mean - uf[:, :, ch], w_pool[g].astype(jnp.float32)))
    return (jnp.concatenate(outs, axis=-1) * pool_scale.astype(jnp.float32)).astype(u.dtype)


def grouped_moe(u, w_router, b_router, w_gate, w_up, w_down):
    s = jax.nn.sigmoid(jnp.einsum('bld,de->ble', u, w_router).astype(jnp.float32))
    sb = s + b_router.astype(jnp.float32)
    grp = sb.reshape(sb.shape[:-1] + (N_GROUPS, EXPERTS_PER_GROUP))
    group_score = jnp.sum(lax.top_k(grp, 2)[0], axis=-1)
    sel_group = jnp.argmax(group_score, axis=-1)
    in_group = (jnp.arange(N_EXPERTS) // EXPERTS_PER_GROUP) == sel_group[..., None]
    _, idx = lax.top_k(jnp.where(in_group, sb, -jnp.inf), TOP_K)
    w_sel = jnp.take_along_axis(s, idx, axis=-1)
    w_sel = w_sel / jnp.sum(w_sel, axis=-1, keepdims=True)
    gates = jnp.sum(jax.nn.one_hot(idx, N_EXPERTS, dtype=jnp.float32) * w_sel[..., None], axis=-2).astype(u.dtype)
    y = jnp.zeros_like(u)
    for e in range(N_EXPERTS):
        h = jax.nn.silu(u @ w_gate[e]) * (u @ w_up[e])
        y = y + gates[..., e:e + 1] * (h @ w_down[e])
    return y


def setup_inputs(seed: int = 0) -> dict:
    key = jax.random.key(seed)
    ks = jax.random.split(key, 21)

    def nrm(k, shape, s):
        return jax.random.normal(k, shape, jnp.float32) * s

    return {
        'x': nrm(ks[0], (BATCH, SEQ, D_MODEL), 1.0),
        'c': nrm(ks[1], (BATCH, D_MODEL), 1.0),
        'ctx': nrm(ks[2], (BATCH, CTX_LEN, D_MODEL), 1.0),
        'c_ctx': nrm(ks[3], (D_MODEL,), 1.0),
        'w_mod': nrm(ks[4], (DEPTH, D_MODEL, 6 * D_MODEL), 0.5 * D_MODEL ** -0.5),
        'b_mod': nrm(ks[5], (DEPTH, 6 * D_MODEL), 0.02),
        'norm1_g': 1.0 + nrm(ks[6], (DEPTH, D_MODEL), 0.05),
        'norm2_g': 1.0 + nrm(ks[7], (DEPTH, D_MODEL), 0.05),
        'w_in': nrm(ks[8], (N_EVEN, D_MODEL, IN_WIDTH), D_MODEL ** -0.5),
        'w_out': nrm(ks[9], (N_EVEN, A_WIDTH + B_WIDTH, D_MODEL), (A_WIDTH + B_WIDTH) ** -0.5),
        'lb_logits': nrm(ks[10], (N_EVEN + 1, 2, A_KW), 0.5),
        'a_norm_g': 1.0 + nrm(ks[11], (N_EVEN, A_WIDTH), 0.05),
        'rpb': nrm(ks[12], (N_EVEN, B_HEADS, 2 * WIN_ROWS - 1, 2 * WIN_COLS - 1), 0.1),
        'w_pool': nrm(ks[13], (N_ODD, len(POOL_WINDOWS), C_GROUP, C_GROUP), C_GROUP ** -0.5),
        'pool_scale': 1.0 + nrm(ks[14], (N_ODD, D_MODEL), 0.1),
        'w_router': nrm(ks[15], (D_MODEL, N_EXPERTS), D_MODEL ** -0.5),
        'b_router': nrm(ks[16], (N_EXPERTS,), 0.01),
        'w_gate': nrm(ks[17], (DEPTH, N_EXPERTS, D_MODEL, D_EXPERT), D_MODEL ** -0.5),
        'w_up': nrm(ks[18], (DEPTH, N_EXPERTS, D_MODEL, D_EXPERT), D_MODEL ** -0.5),
        'w_down': nrm(ks[19], (DEPTH, N_EXPERTS, D_EXPERT, D_MODEL), D_EXPERT ** -0.5),
        'final_g': 1.0 + nrm(ks[20], (D_MODEL,), 0.05),
    }


def reference(x, c, ctx, c_ctx, w_mod, b_mod, norm1_g, norm2_g, w_in, w_out, lb_logits, a_norm_g,
              rpb, w_pool, pool_scale, w_router, b_router, w_gate, w_up, w_down, final_g):
    last_even = ((DEPTH - 1) // 2) * 2
    lbs = hgrn_lower_bounds(lb_logits)
    cond_l = c[:, None, :]
    cond_c = c_ctx[None, None, :]
    h_c, h_l = ctx, x
    for layer in range(DEPTH):
        j = layer // 2
        ctx_in = layer <= last_even
        ctx_out = layer < last_even
        sh1, sc1, g1, sh2, sc2, g2 = adaln(cond_l, w_mod[layer], b_mod[layer])
        u_l = modulate(rms_norm(h_l, norm1_g[layer]), sh1, sc1)
        if ctx_in:
            mc = adaln(cond_c, w_mod[layer], b_mod[layer])
            u_c = modulate(rms_norm(h_c, norm1_g[layer]), mc[0], mc[1])
        if layer % 2 == 0:
            y_c, y_l = even_mixer(u_c, u_l, w_in[j], w_out[j], lbs[j], a_norm_g[j], rpb[j], ctx_out)
        else:
            y_l = pool_mixer(u_l, w_pool[j], pool_scale[j])
            y_c = pool_mixer(u_c, w_pool[j], pool_scale[j]) if ctx_out else None
        h_l = h_l + g1 * y_l
        h_l = h_l + g2 * grouped_moe(modulate(rms_norm(h_l, norm2_g[layer]), sh2, sc2),
                                     w_router, b_router, w_gate[layer], w_up[layer], w_down[layer])
        if ctx_out:
            h_c = h_c + mc[2] * y_c
            h_c = h_c + mc[5] * grouped_moe(modulate(rms_norm(h_c, norm2_g[layer]), mc[3], mc[4]),
                                            w_router, b_router, w_gate[layer], w_up[layer], w_down[layer])
    return rms_norm(h_l, final_g)
```

```python
import functools

import numpy as np
import jax
import jax.numpy as jnp
from jax import lax
from jax.experimental import pallas as pl
from jax.experimental.pallas import tpu as pltpu

F32 = jnp.float32
BF16 = jnp.bfloat16
HIGHEST = lax.Precision.HIGHEST

EPS = 1e-6
GRID_W = 64
A_HEADS = 4
A_DK = 128
A_W = A_HEADS * A_DK
SCAN_CHUNK = 64
SCAN_LEVELS = 6
B_HEADS = 8
B_DH = 64
B_W = B_HEADS * B_DH
WIN_ROWS = 8
WIN_COLS = 16
Q_ROWS = 4
K_ROWS = Q_ROWS + WIN_ROWS
COL_QA, COL_FF, COL_FB, COL_IA, COL_GA, COL_QB, COL_KB, COL_VB = range(8)
IN_WIDTH = 8 * A_W
POOL_WINDOWS = (2, 4, 8, 16)
POOL_HALO = 8
N_EXPERTS = 16
N_GROUPS = 4
EPG = N_EXPERTS // N_GROUPS
PAIRS = ((0, 1), (0, 2), (0, 3), (1, 2), (1, 3), (2, 3))
N_CLASSES = N_GROUPS * len(PAIRS)
D_EXPERT = 512

ROW_TILE = 256
MOE_TILE = 256
NEG_BIG = -1e30
VMEM_LIMIT = 56 * 1024 * 1024


def _cparams(sem):
    return pltpu.CompilerParams(dimension_semantics=sem, vmem_limit_bytes=VMEM_LIMIT)


def _dot(a, b):
    return jnp.dot(a, b, preferred_element_type=F32)


def _dot_nt(a, b):
    return lax.dot_general(a, b, (((1,), (1,)), ((), ())), preferred_element_type=F32)


def _dot_tn(a, b):
    return lax.dot_general(a, b, (((0,), (0,)), ((), ())), preferred_element_type=F32)


def _rms_mod(x, g, shift, scale):
    ms = jnp.mean(x * x, axis=-1, keepdims=True)
    return (x * lax.rsqrt(ms + EPS) * g) * (1.0 + scale) + shift


def _mod_kernel(cond_ref, w_ref, b_ref, o_ref):
    c = cond_ref[...]
    a = c * jax.nn.sigmoid(c)
    o_ref[0] = jnp.dot(a, w_ref[0], precision=HIGHEST, preferred_element_type=F32) + b_ref[0]


def _adaln(cond8, w_mod, b_mod):
    depth, d, n6 = w_mod.shape
    bn = 1536
    return pl.pallas_call(
        _mod_kernel,
        out_shape=jax.ShapeDtypeStruct((depth, 8, n6), F32),
        grid=(depth, n6 // bn),
        in_specs=[pl.BlockSpec((8, d), lambda l, j: (0, 0)),
                  pl.BlockSpec((1, d, bn), lambda l, j: (l, 0, j)),
                  pl.BlockSpec((1, 1, bn), lambda l, j: (l, 0, j))],
        out_specs=pl.BlockSpec((1, 8, bn), lambda l, j: (l, 0, j)),
        compiler_params=_cparams(("arbitrary", "arbitrary")),
        name="adaln",
    )(cond8, w_mod, b_mod.reshape(depth, 1, n6))


def _inproj_kernel(x_ref, g_ref, sh_ref, sc_ref, w_ref, *rest):
    o_ref = rest[-1]
    u = _rms_mod(x_ref[...], g_ref[...], sh_ref[...], sc_ref[...]).astype(BF16)
    for j in range(IN_WIDTH // A_W):
        o_ref[:, j * A_W:(j + 1) * A_W] = _dot(u, w_ref[:, j * A_W:(j + 1) * A_W]).astype(BF16)


def _inproj(x2d, g, sh, sc, w_bf, total_rows, row_block_off, prev=None):
    n, d = x2d.shape
    vec = pl.BlockSpec((1, d), lambda i: (0, 0))
    in_specs = [pl.BlockSpec((ROW_TILE, d), lambda i: (i, 0)), vec, vec, vec,
                pl.BlockSpec((d, IN_WIDTH), lambda i: (0, 0))]
    args = [x2d, g, sh, sc, w_bf]
    aliases = {}
    if prev is not None:
        in_specs.append(pl.BlockSpec(memory_space=pl.ANY))
        args.append(prev)
        aliases = {5: 0}
    return pl.pallas_call(
        _inproj_kernel,
        out_shape=jax.ShapeDtypeStruct((total_rows, IN_WIDTH), BF16),
        grid=(n // ROW_TILE,),
        in_specs=in_specs,
        out_specs=pl.BlockSpec((ROW_TILE, IN_WIDTH), lambda i: (i + row_block_off, 0)),
        input_output_aliases=aliases,
        compiler_params=_cparams(("arbitrary",)),
        name="inproj",
    )(*args)


def _scan_constants(reverse):
    c = SCAN_CHUNK
    pos = np.arange(c)[::-1] if reverse else np.arange(c)
    cum = (pos[None, :] <= pos[:, None]).astype(np.float32)
    sel = [cum]
    sgn = []
    mask = [np.eye(c, dtype=np.float32)]
    for l in range(SCAN_LEVELS):
        b = 1 << l
        pair = pos // (2 * b)
        late = (pos // b) % 2 == 1
        ref_pos = pair * 2 * b + b - 1
        sel.append((pos[None, :] <= ref_pos[:, None]).astype(np.float32))
        sgn.append(np.broadcast_to(np.where(late, 1.0, -1.0).astype(np.float32)[:, None], (c, A_DK)))
        mask.append((late[:, None] & ~late[None, :] & (pair[:, None] == pair[None, :])).astype(np.float32))
    sel.append(np.ones((c, c), np.float32))
    return (jnp.asarray(np.concatenate(sel, 0), BF16), jnp.asarray(np.stack(sgn)),
            jnp.asarray(np.stack(mask)))


def _hgrn_kernel(*refs, reverse, final):
    if final:
        (q_ref, f_ref, v_ref, lb_ref, sel_ref, sgn_ref, mask_ref, g_ref, prev_ref, ng_ref,
         o_ref, st_ref) = refs
    else:
        q_ref, f_ref, v_ref, lb_ref, sel_ref, sgn_ref, mask_ref, o_ref, st_ref = refs
    c = SCAN_CHUNK
    n_chunks = q_ref.shape[0] // c

    @pl.when(pl.program_id(0) == 0)
    def _():
        st_ref[...] = jnp.zeros_like(st_ref)

    lb = lb_ref[...]
    sel = sel_ref[...]
    q_scale = A_DK ** -0.5

    def chunk(ci, carry):
        idx = (n_chunks - 1 - ci) if reverse else ci
        rows = pl.ds(pl.multiple_of(idx * c, c), c)
        q = q_ref[rows, :].astype(F32) * q_scale
        fr = f_ref[rows, :].astype(F32)
        v = v_ref[rows, :].astype(BF16)
        f = lb + (1.0 - lb) * jax.nn.sigmoid(fr)
        k = 1.0 - f
        lf = jnp.log(f)
        hi = lf.astype(BF16)
        lo = (lf - hi.astype(F32)).astype(BF16)
        cums = _dot(sel, hi) + _dot(sel, lo)
        outs = []
        for h in range(A_HEADS):
            sl = slice(h * A_DK, (h + 1) * A_DK)
            qh, kh, vh = q[:, sl], k[:, sl], v[:, sl]
            bc = cums[0:c, sl]
            tot = cums[(SCAN_LEVELS + 1) * c:(SCAN_LEVELS + 2) * c, sl]
            scores = _dot_nt(qh.astype(BF16), kh.astype(BF16)) * mask_ref[0]
            for l in range(SCAN_LEVELS):
                beta = cums[(l + 1) * c:(l + 2) * c, sl]
                e = jnp.exp((bc - beta) * sgn_ref[l])
                scores = scores + _dot_nt((qh * e).astype(BF16), (kh * e).astype(BF16)) * mask_ref[l + 1]
            st = st_ref[h]
            o = _dot(scores.astype(BF16), vh)
            o = o + _dot_nt((qh * jnp.exp(bc)).astype(BF16), st.astype(BF16))
            kd = (kh * jnp.exp(tot - bc)).astype(BF16)
            st_ref[h] = st * jnp.exp(tot[0:1, :]) + _dot_tn(vh, kd)
            outs.append(o)
        o = jnp.concatenate(outs, axis=1)
        if final:
            o = o + prev_ref[rows, :]
            normed = []
            for h in range(A_HEADS):
                oh = o[:, h * A_DK:(h + 1) * A_DK]
                normed.append(oh * lax.rsqrt(jnp.mean(oh * oh, axis=-1, keepdims=True) + EPS))
            g = g_ref[rows, :].astype(F32)
            o = jnp.concatenate(normed, axis=1) * ng_ref[...] * (g * jax.nn.sigmoid(g))
            o_ref[rows, :] = o.astype(o_ref.dtype)
        else:
            o_ref[rows, :] = o
        return carry

    lax.fori_loop(0, n_chunks, chunk, 0)


def _hgrn_pass(proj, lb, consts, reverse, prev=None, norm_g=None):
    n = proj.shape[0]
    nb = n // ROW_TILE
    final = prev is not None
    sel, sgn, mask = consts
    if reverse:
        blk = lambda i: jnp.where(i == 0, 0, nb - i)
    else:
        blk = lambda i: i
    col = lambda cidx: pl.BlockSpec((ROW_TILE, A_W), lambda i: (blk(i), cidx))
    const = lambda a: pl.BlockSpec(a.shape, lambda i: (0,) * a.ndim)
    in_specs = [col(COL_QA), col(COL_FB if reverse else COL_FF), col(COL_IA), const(lb),
                const(sel), const(sgn), const(mask)]
    args = [proj, proj, proj, lb, sel, sgn, mask]
    if final:
        in_specs += [col(COL_GA), pl.BlockSpec((ROW_TILE, A_W), lambda i: (blk(i), 0)), const(norm_g)]
        args += [proj, prev, norm_g]
    return pl.pallas_call(
        functools.partial(_hgrn_kernel, reverse=reverse, final=final),
        out_shape=jax.ShapeDtypeStruct((n, A_W), BF16 if final else F32),
        grid=(nb,),
        in_specs=in_specs,
        out_specs=pl.BlockSpec((ROW_TILE, A_W), lambda i: (blk(i), 0)),
        scratch_shapes=[pltpu.VMEM((A_HEADS, A_DK, A_DK), F32)],
        compiler_params=_cparams(("arbitrary",)),
        name="hgrn_bwd" if reverse else "hgrn_fwd",
    )(*args)


def _natten_bias(rpb, rows):
    w = GRID_W
    n_tiles = rows // Q_ROWS
    cols = np.arange(w)
    wc0 = np.clip(cols - WIN_COLS // 2, 0, w - WIN_COLS)
    col_ok = (cols[None, :] >= wc0[:, None]) & (cols[None, :] < wc0[:, None] + WIN_COLS)
    dc = np.clip(cols[None, :] - cols[:, None] + WIN_COLS - 1, 0, 2 * WIN_COLS - 2)
    out = []
    for tile in (0, 1, n_tiles - 1):
        r = tile * Q_ROWS + np.arange(Q_ROWS)
        start = np.clip(tile * Q_ROWS - WIN_ROWS // 2, 0, rows - K_ROWS)
        kr = start + np.arange(K_ROWS)
        wr0 = np.clip(r - WIN_ROWS // 2, 0, rows - WIN_ROWS)
        row_ok = (kr[None, :] >= wr0[:, None]) & (kr[None, :] < wr0[:, None] + WIN_ROWS)
        dr = np.clip(kr[None, :] - r[:, None] + WIN_ROWS - 1, 0, 2 * WIN_ROWS - 2)
        ok = row_ok[:, None, :, None] & col_ok[None, :, None, :]
        vals = rpb[:, dr[:, None, :, None], dc[None, :, None, :]]
        vals = jnp.where(jnp.asarray(ok)[None], vals.astype(F32), NEG_BIG)
        out.append(vals.reshape(B_HEADS, Q_ROWS * w, K_ROWS * w))
    return jnp.stack(out)


def _natten_kernel(q_ref, k0_ref, k1_ref, k2_ref, v0_ref, v1_ref, v2_ref, kc_ref, vc_ref, bias_ref, o_ref):
    kv = ((k0_ref, v0_ref), (k1_ref, v1_ref), (k2_ref, v2_ref))
    blk = k0_ref.shape[0]
    lane = lax.broadcasted_iota(jnp.int32, (1, 2 * B_DH), 1)
    scale = B_DH ** -0.5
    for pair in range(B_HEADS // 2):
        sl = slice(pair * 2 * B_DH, (pair + 1) * 2 * B_DH)
        q = q_ref[:, sl]
        res = []
        for sub in range(2):
            h = 2 * pair + sub
            own = (lane >= sub * B_DH) & (lane < (sub + 1) * B_DH)
            qm = jnp.where(own, q, jnp.zeros_like(q)) * jnp.asarray(scale, BF16)
            s = [_dot_nt(qm, kr[:, sl]) + bias_ref[0, h, :, j * blk:(j + 1) * blk]
                 for j, (kr, _) in enumerate(kv)]
            s.append(_dot_nt(qm, kc_ref[:, sl]))
            m = functools.reduce(jnp.maximum, [jnp.max(x, axis=-1, keepdims=True) for x in s])
            p = [jnp.exp(x - m) for x in s]
            den = functools.reduce(jnp.add, [jnp.sum(x, axis=-1, keepdims=True) for x in p])
            acc = _dot(p[-1].astype(BF16), vc_ref[:, sl])
            for j, (_, vr) in enumerate(kv):
                acc = acc + _dot(p[j].astype(BF16), vr[:, sl])
            res.append(acc / den)
        o_ref[:, sl] = jnp.where(lane < B_DH, res[0], res[1]).astype(o_ref.dtype)


def _natten(proj, bias, n_ctx, n_lat):
    rows = n_lat // GRID_W
    n_tiles = rows // Q_ROWS
    qblk = Q_ROWS * GRID_W
    assert n_ctx == qblk and qblk == ROW_TILE and n_tiles >= 3
    off = n_ctx // qblk
    start = lambda i: jnp.clip(i - 1, 0, n_tiles - 3) + off
    variant = lambda i: jnp.where(i == 0, 0, jnp.where(i == n_tiles - 1, 2, 1))
    kvspec = lambda cidx, j: pl.BlockSpec((qblk, B_W), lambda i: (start(i) + j, cidx))
    in_specs = ([pl.BlockSpec((qblk, B_W), lambda i: (i + off, COL_QB))]
                + [kvspec(COL_KB, j) for j in range(3)] + [kvspec(COL_VB, j) for j in range(3)]
                + [pl.BlockSpec((n_ctx, B_W), lambda i: (0, COL_KB)),
                   pl.BlockSpec((n_ctx, B_W), lambda i: (0, COL_VB)),
                   pl.BlockSpec((1, B_HEADS, qblk, K_ROWS * GRID_W), lambda i: (variant(i), 0, 0, 0))])
    return pl.pallas_call(
        _natten_kernel,
        out_shape=jax.ShapeDtypeStruct((n_lat, B_W), BF16),
        grid=(n_tiles,),
        in_specs=in_specs,
        out_specs=pl.BlockSpec((qblk, B_W), lambda i: (i, 0)),
        compiler_params=_cparams(("arbitrary",)),
        name="natten",
    )(*([proj] * 9 + [bias]))


def _route(u, wr_t, b_col):
    logits = lax.dot_general(wr_t, u, (((1,), (1,)), ((), ())), precision=HIGHEST,
                             preferred_element_type=F32)
    s = jax.nn.sigmoid(logits)
    sb = s + b_col
    srow = [s[i:i + 1] for i in range(N_EXPERTS)]
    brow = [sb[i:i + 1] for i in range(N_EXPERTS)]

    def top2sum(a, b, c, d):
        hi1, lo1, hi2, lo2 = jnp.maximum(a, b), jnp.minimum(a, b), jnp.maximum(c, d), jnp.minimum(c, d)
        return jnp.maximum(hi1, hi2) + jnp.maximum(jnp.minimum(hi1, hi2), jnp.maximum(lo1, lo2))

    gs = [top2sum(*brow[EPG * g:EPG * (g + 1)]) for g in range(N_GROUPS)]
    best, gsel = gs[0], jnp.zeros_like(gs[0], dtype=jnp.int32)
    for g in range(1, N_GROUPS):
        upd = gs[g] > best
        gsel = jnp.where(upd, g, gsel)
        best = jnp.where(upd, gs[g], best)

    def pick(rows_, j):
        out = rows_[j]
        for g in range(1, N_GROUPS):
            out = jnp.where(gsel == g, rows_[EPG * g + j], out)
        return out

    bg = [pick(brow, j) for j in range(EPG)]
    sg = [pick(srow, j) for j in range(EPG)]

    def first_argmax(vals):
        m = functools.reduce(jnp.maximum, vals)
        idx = jnp.full_like(gsel, EPG - 1)
        for j in range(EPG - 2, -1, -1):
            idx = jnp.where(vals[j] == m, j, idx)
        return idx

    i1 = first_argmax(bg)
    i2 = first_argmax([jnp.where(i1 == j, -jnp.inf, bg[j]) for j in range(EPG)])

    def take(vals, idx):
        out = vals[0]
        for j in range(1, EPG):
            out = jnp.where(idx == j, vals[j], out)
        return out

    w1, w2 = take(sg, i1), take(sg, i2)
    wsum = w1 + w2
    n1, n2 = w1 / wsum, w2 / wsum
    ia, ib = jnp.minimum(i1, i2), jnp.maximum(i1, i2)
    first_low = i1 < i2
    ga, gb = jnp.where(first_low, n1, n2), jnp.where(first_low, n2, n1)
    pair_off = jnp.where(ia == 0, 0, jnp.where(ia == 1, 3, 5))
    cls = gsel * len(PAIRS) + pair_off + ib - ia - 1
    zeros = jnp.zeros_like(ga)
    return jnp.concatenate([cls.astype(F32), ga, gb] + [zeros] * 5, axis=0)


def _outproj_kernel(a_ref, b_ref, wa_ref, wb_ref, x_ref, g1_ref, n2_ref, sh2_ref, sc2_ref, wr_ref, br_ref,
                    h_ref, u_ref, r_ref):
    y = _dot(a_ref[...], wa_ref[...]) + _dot(b_ref[...], wb_ref[...])
    h = x_ref[...] + g1_ref[...] * y
    h_ref[...] = h
    u = _rms_mod(h, n2_ref[...], sh2_ref[...], sc2_ref[...])
    u_ref[...] = u.astype(BF16)
    r_ref[...] = _route(u, wr_ref[...], br_ref[...])


def _outproj(a_all, b_l, w_out_bf, x2d, g1, n2, sh2, sc2, wr_t, br_col, row_block_off):
    n, d = x2d.shape
    vec = pl.BlockSpec((1, d), lambda i: (0, 0))
    const = lambda a: pl.BlockSpec(a.shape, lambda i: (0,) * a.ndim)
    return pl.pallas_call(
        _outproj_kernel,
        out_shape=(jax.ShapeDtypeStruct((n, d), F32), jax.ShapeDtypeStruct((n, d), BF16),
                   jax.ShapeDtypeStruct((8, n), F32)),
        grid=(n // ROW_TILE,),
        in_specs=[pl.BlockSpec((ROW_TILE, A_W), lambda i: (i + row_block_off, 0)),
                  pl.BlockSpec((ROW_TILE, B_W), lambda i: (i, 0)),
                  pl.BlockSpec((A_W, d), lambda i: (0, 0)),
                  pl.BlockSpec((B_W, d), lambda i: (1, 0)),
                  pl.BlockSpec((ROW_TILE, d), lambda i: (i, 0)),
                  vec, vec, vec, vec, const(wr_t), const(br_col)],
        out_specs=(pl.BlockSpec((ROW_TILE, d), lambda i: (i, 0)),
                   pl.BlockSpec((ROW_TILE, d), lambda i: (i, 0)),
                   pl.BlockSpec((8, ROW_TILE), lambda i: (0, i))),
        compiler_params=_cparams(("arbitrary",)),
        name="outproj",
    )(a_all, b_l, w_out_bf, w_out_bf, x2d, g1, n2, sh2, sc2, wr_t, br_col)


def _moe_kernel(ea_ref, eb_ref, ok_ref, x_ref, ga_ref, gb_ref, wga_ref, wua_ref, wda_ref,
                wgb_ref, wub_ref, wdb_ref, o_ref):
    j = pl.program_id(0)

    @pl.when(ok_ref[j] == 0)
    def _():
        o_ref[...] = jnp.zeros_like(o_ref)

    @pl.when(ok_ref[j] != 0)
    def _():
        x = x_ref[...]

        def hidden(wg_ref, wu_ref, gate_ref):
            a = _dot(x, wg_ref[0])
            return ((a * jax.nn.sigmoid(a)) * _dot(x, wu_ref[0]) * gate_ref[...]).astype(BF16)

        y = _dot(hidden(wga_ref, wua_ref, ga_ref), wda_ref[0])
        y = y + _dot(hidden(wgb_ref, wub_ref, gb_ref), wdb_ref[0])
        o_ref[...] = y.astype(o_ref.dtype)


def _moe(x_sorted, ga_s, gb_s, tile_ea, tile_eb, tile_ok, wg, wu, wd):
    p, d = x_sorted.shape
    nt = p // MOE_TILE
    wspec_in = lambda tbl: pl.BlockSpec((1, d, D_EXPERT), lambda j, ea, eb, ok: ((ea, eb)[tbl][j], 0, 0))
    wspec_out = lambda tbl: pl.BlockSpec((1, D_EXPERT, d), lambda j, ea, eb, ok: ((ea, eb)[tbl][j], 0, 0))
    gspec = pl.BlockSpec((MOE_TILE, 1), lambda j, ea, eb, ok: (j, 0))
    grid_spec = pltpu.PrefetchScalarGridSpec(
        num_scalar_prefetch=3,
        grid=(nt,),
        in_specs=[pl.BlockSpec((MOE_TILE, d), lambda j, ea, eb, ok: (j, 0)), gspec, gspec,
                  wspec_in(0), wspec_in(0), wspec_out(0), wspec_in(1), wspec_in(1), wspec_out(1)],
        out_specs=pl.BlockSpec((MOE_TILE, d), lambda j, ea, eb, ok: (j, 0)),
    )
    return pl.pallas_call(
        _moe_kernel,
        out_shape=jax.ShapeDtypeStruct((p, d), BF16),
        grid_spec=grid_spec,
        compiler_params=_cparams(("arbitrary",)),
        name="moe",
    )(tile_ea, tile_eb, tile_ok, x_sorted, ga_s, gb_s, wg, wu, wd, wg, wu, wd)


def _sort_tables(route, n_tiles):
    t = route.shape[1]
    cls = route[0].astype(jnp.int32)
    onehot = (cls[:, None] == jnp.arange(N_CLASSES, dtype=jnp.int32)[None, :]).astype(jnp.int32)
    csum = jnp.cumsum(onehot, axis=0)
    counts = csum[-1]
    rank = jnp.sum((csum - 1) * onehot, axis=1)
    padded = ((counts + MOE_TILE - 1) // MOE_TILE) * MOE_TILE
    ends = jnp.cumsum(padded)
    starts = ends - padded
    dest = jnp.sum(starts[None, :] * onehot, axis=1) + rank
    total = ends[-1]
    tile_start = jnp.arange(n_tiles, dtype=jnp.int32) * MOE_TILE
    tile_ok = (tile_start < total).astype(jnp.int32)
    tile_cls = jnp.sum((ends[None, :] <= jnp.minimum(tile_start, total - 1)[:, None]).astype(jnp.int32), axis=1)
    tile_cls = jnp.clip(tile_cls, 0, N_CLASSES - 1)
    pair = jnp.asarray(np.array(PAIRS, np.int32))
    group, pidx = tile_cls // len(PAIRS), tile_cls % len(PAIRS)
    tile_ea = group * EPG + pair[pidx, 0]
    tile_eb = group * EPG + pair[pidx, 1]
    p = n_tiles * MOE_TILE
    src = jnp.zeros((p,), jnp.int32).at[dest].set(jnp.arange(t, dtype=jnp.int32))
    ga_s = jnp.zeros((p,), F32).at[dest].set(route[1])
    gb_s = jnp.zeros((p,), F32).at[dest].set(route[2])
    return dest, src, ga_s[:, None], gb_s[:, None], tile_ea, tile_eb, tile_ok


def _moe_layer(u_bf, route, wg, wu, wd):
    t = u_bf.shape[0]
    n_tiles = (t + N_CLASSES * (MOE_TILE - 1)) // MOE_TILE + 1
    dest, src, ga_s, gb_s, tile_ea, tile_eb, tile_ok = _sort_tables(route, n_tiles)
    x_sorted = jnp.take(u_bf, src, axis=0)
    y_sorted = _moe(x_sorted, ga_s, gb_s, tile_ea, tile_eb, tile_ok, wg, wu, wd)
    return jnp.take(y_sorted, dest, axis=0)


def _resid_kernel(h_ref, y_ref, g_ref, o_ref):
    o_ref[...] = h_ref[...] + g_ref[...] * y_ref[...].astype(F32)


def _resid(h, y, g):
    n, d = h.shape
    row = pl.BlockSpec((ROW_TILE, d), lambda i: (i, 0))
    return pl.pallas_call(
        _resid_kernel, out_shape=jax.ShapeDtypeStruct((n, d), F32), grid=(n // ROW_TILE,),
        in_specs=[row, row, pl.BlockSpec((1, d), lambda i: (0, 0))], out_specs=row,
        compiler_params=_cparams(("arbitrary",)), name="resid",
    )(h, y, g)


def _pool_kernel(prev_ref, cur_ref, next_ref, n1_ref, sh1_ref, sc1_ref, wp_ref, ps_ref, g1_ref,
                 n2_ref, sh2_ref, sc2_ref, wr_ref, br_ref, h_ref, u_ref, r_ref, ext_ref, *, n_tokens):
    i = pl.program_id(0)
    tm, d = cur_ref.shape
    hal = POOL_HALO
    mod = lambda x: _rms_mod(x, n1_ref[...], sh1_ref[...], sc1_ref[...])
    h = cur_ref[...]
    u = mod(h)
    ext_ref[0:hal, :] = mod(prev_ref[...]) * (i > 0).astype(F32)
    ext_ref[hal:hal + tm, :] = u
    ext_ref[hal + tm:hal + tm + hal, :] = mod(next_ref[...]) * (i < pl.num_programs(0) - 1).astype(F32)
    t = i * tm + lax.broadcasted_iota(jnp.int32, (tm, 1), 0)
    cg = d // len(POOL_WINDOWS)
    outs = []
    for g, win in enumerate(POOL_WINDOWS):
        cs = slice(g * cg, (g + 1) * cg)
        acc = ext_ref[hal - win // 2:hal - win // 2 + tm, cs]
        for dlt in range(-win // 2 + 1, win // 2):
            acc = acc + ext_ref[hal + dlt:hal + dlt + tm, cs]
        cnt = (jnp.minimum(t + win // 2, n_tokens) - jnp.maximum(t - win // 2, 0)).astype(F32)
        z = acc / cnt - u[:, cs]
        outs.append(_dot(z.astype(BF16), wp_ref[g]))
    y = jnp.concatenate(outs, axis=1) * ps_ref[...]
    h = h + g1_ref[...] * y
    h_ref[...] = h
    u2 = _rms_mod(h, n2_ref[...], sh2_ref[...], sc2_ref[...])
    u_ref[...] = u2.astype(BF16)
    r_ref[...] = _route(u2, wr_ref[...], br_ref[...])


def _pool(h, n1, sh1, sc1, wp_bf, ps, g1, n2, sh2, sc2, wr_t, br_col):
    n, d = h.shape
    per = ROW_TILE // POOL_HALO
    vec = pl.BlockSpec((1, d), lambda i: (0, 0))
    const = lambda a: pl.BlockSpec(a.shape, lambda i: (0,) * a.ndim)
    row = pl.BlockSpec((ROW_TILE, d), lambda i: (i, 0))
    return pl.pallas_call(
        functools.partial(_pool_kernel, n_tokens=n),
        out_shape=(jax.ShapeDtypeStruct((n, d), F32), jax.ShapeDtypeStruct((n, d), BF16),
                   jax.ShapeDtypeStruct((8, n), F32)),
        grid=(n // ROW_TILE,),
        in_specs=[pl.BlockSpec((POOL_HALO, d), lambda i: (jnp.maximum(i * per - 1, 0), 0)),
                  row,
                  pl.BlockSpec((POOL_HALO, d), lambda i: (jnp.minimum((i + 1) * per, n // POOL_HALO - 1), 0)),
                  vec, vec, vec, const(wp_bf), vec, vec, vec, vec, vec, const(wr_t), const(br_col)],
        out_specs=(row, row, pl.BlockSpec((8, ROW_TILE), lambda i: (0, i))),
        scratch_shapes=[pltpu.VMEM((ROW_TILE + 2 * POOL_HALO, d), F32)],
        compiler_params=_cparams(("arbitrary",)),
        name="pool",
    )(h, h, h, n1, sh1, sc1, wp_bf, ps, g1, n2, sh2, sc2, wr_t, br_col)


def _final_kernel(h_ref, y_ref, g_ref, fg_ref, o_ref):
    h = h_ref[...] + g_ref[...] * y_ref[...].astype(F32)
    o_ref[...] = h * lax.rsqrt(jnp.mean(h * h, axis=-1, keepdims=True) + EPS) * fg_ref[...]


def _final(h, y, g, fg):
    n, d = h.shape
    row = pl.BlockSpec((ROW_TILE, d), lambda i: (i, 0))
    vec = pl.BlockSpec((1, d), lambda i: (0, 0))
    return pl.pallas_call(
        _final_kernel, out_shape=jax.ShapeDtypeStruct((n, d), F32), grid=(n // ROW_TILE,),
        in_specs=[row, row, vec, vec], out_specs=row,
        compiler_params=_cparams(("arbitrary",)), name="final",
    )(h, y, g, fg)


def _lower_bounds(lb_logits):
    p = jax.nn.softmax(lb_logits.astype(F32), axis=0)
    return jnp.cumsum(p, axis=0)[1:] - p[0]


def kernel(x, c, ctx, c_ctx, w_mod, b_mod, norm1_g, norm2_g, w_in, w_out, lb_logits, a_norm_g, rpb,
           w_pool, pool_scale, w_router, b_router, w_gate, w_up, w_down, final_g):
    batch, n_lat, d = x.shape
    n_ctx = ctx.shape[1]
    assert batch == 1 and w_mod.shape[0] == 2 and n_lat % ROW_TILE == 0 and n_ctx % ROW_TILE == 0
    x2d, ctx2d = x[0], ctx[0]
    row = lambda v: v.reshape(1, -1)

    cond8 = jnp.zeros((8, d), F32).at[0].set(c[0]).at[1].set(c_ctx)
    mod = _adaln(cond8, w_mod, b_mod)
    mvec = lambda layer, r, k: mod[layer, r:r + 1, k * d:(k + 1) * d]
    sh1, sc1, g1, sh2, sc2, g2 = (mvec(0, 0, k) for k in range(6))
    csh1, csc1 = mvec(0, 1, 0), mvec(0, 1, 1)

    w_in_bf = w_in[0].astype(BF16)
    total = n_ctx + n_lat
    off = n_ctx // ROW_TILE
    proj = _inproj(x2d, row(norm1_g[0]), sh1, sc1, w_in_bf, total, off)
    proj = _inproj(ctx2d, row(norm1_g[0]), csh1, csc1, w_in_bf, total, 0, prev=proj)

    lbs = _lower_bounds(lb_logits)[0]
    o_fwd = _hgrn_pass(proj, row(lbs[0]), _scan_constants(False), reverse=False)
    a_all = _hgrn_pass(proj, row(lbs[1]), _scan_constants(True), reverse=True, prev=o_fwd,
                       norm_g=row(a_norm_g[0]))
    b_l = _natten(proj, _natten_bias(rpb[0], n_lat // GRID_W), n_ctx, n_lat)

    wr_t = w_router.T
    br_col = b_router.reshape(-1, 1)
    h, u, route = _outproj(a_all, b_l, w_out[0].astype(BF16), x2d, g1, row(norm2_g[0]), sh2, sc2,
                           wr_t, br_col, off)
    y = _moe_layer(u, route, w_gate[0].astype(BF16), w_up[0].astype(BF16), w_down[0].astype(BF16))
    h = _resid(h, y, g2)

    sh1, sc1, g1, sh2, sc2, g2 = (mvec(1, 0, k) for k in range(6))
    h, u, route = _pool(h, row(norm1_g[1]), sh1, sc1, w_pool[0].astype(BF16), row(pool_scale[0]), g1,
                        row(norm2_g[1]), sh2, sc2, wr_t, br_col)
    y = _moe_layer(u, route, w_gate[1].astype(BF16), w_up[1].astype(BF16), w_down[1].astype(BF16))
    return _final(h, y, g2, row(final_g))[None]
```

```python
import functools

import numpy as np
import jax
import jax.numpy as jnp
from jax import lax
from jax.experimental import pallas as pl
from jax.experimental.pallas import tpu as pltpu

F32 = jnp.float32
BF16 = jnp.bfloat16
HIGHEST = lax.Precision.HIGHEST

EPS = 1e-6
GRID_W = 64
A_HEADS = 4
A_DK = 128
A_W = A_HEADS * A_DK
SCAN_CHUNK = 64
SCAN_LEVELS = 6
B_HEADS = 8
B_DH = 64
B_W = B_HEADS * B_DH
WIN_ROWS = 8
WIN_COLS = 16
Q_ROWS = 4
K_ROWS = Q_ROWS + WIN_ROWS
COL_QA, COL_FF, COL_FB, COL_IA, COL_GA, COL_QB, COL_KB, COL_VB = range(8)
IN_WIDTH = 8 * A_W
POOL_WINDOWS = (2, 4, 8, 16)
POOL_HALO = 8
N_EXPERTS = 16
N_GROUPS = 4
EPG = N_EXPERTS // N_GROUPS
PAIRS = ((0, 1), (0, 2), (0, 3), (1, 2), (1, 3), (2, 3))
N_CLASSES = N_GROUPS * len(PAIRS)
D_EXPERT = 512

ROW_TILE = 256
MOE_TILE = 256
GATHER_ROWS = 256
NEG_BIG = -1e30
VMEM_LIMIT = 56 * 1024 * 1024


def _cparams(sem):
    return pltpu.CompilerParams(dimension_semantics=sem, vmem_limit_bytes=VMEM_LIMIT)


def _dot(a, b):
    return jnp.dot(a, b, preferred_element_type=F32)


def _dot_nt(a, b):
    return lax.dot_general(a, b, (((1,), (1,)), ((), ())), preferred_element_type=F32)


def _dot_tn(a, b):
    return lax.dot_general(a, b, (((0,), (0,)), ((), ())), preferred_element_type=F32)


def _rms_mod(x, g, shift, scale):
    ms = jnp.mean(x * x, axis=-1, keepdims=True)
    return (x * lax.rsqrt(ms + EPS) * g) * (1.0 + scale) + shift


def _mod_kernel(cond_ref, w_ref, b_ref, o_ref):
    c = cond_ref[...]
    a = c * jax.nn.sigmoid(c)
    o_ref[0] = jnp.dot(a, w_ref[0], precision=HIGHEST, preferred_element_type=F32) + b_ref[0]


def _adaln(cond8, w_mod, b_mod):
    depth, d, n6 = w_mod.shape
    bn = 1536
    return pl.pallas_call(
        _mod_kernel,
        out_shape=jax.ShapeDtypeStruct((depth, 8, n6), F32),
        grid=(depth, n6 // bn),
        in_specs=[pl.BlockSpec((8, d), lambda l, j: (0, 0)),
                  pl.BlockSpec((1, d, bn), lambda l, j: (l, 0, j)),
                  pl.BlockSpec((1, 1, bn), lambda l, j: (l, 0, j))],
        out_specs=pl.BlockSpec((1, 8, bn), lambda l, j: (l, 0, j)),
        compiler_params=_cparams(("arbitrary", "arbitrary")),
        name="adaln",
    )(cond8, w_mod, b_mod.reshape(depth, 1, n6))


def _inproj_kernel(ctx_ref, x_ref, g_ref, csh_ref, csc_ref, sh_ref, sc_ref, w_ref, o_ref, *, n_ctx_blocks):
    is_ctx = pl.program_id(0) < n_ctx_blocks
    x = jnp.where(is_ctx, ctx_ref[...], x_ref[...])
    sh = jnp.where(is_ctx, csh_ref[...], sh_ref[...])
    sc = jnp.where(is_ctx, csc_ref[...], sc_ref[...])
    u = _rms_mod(x, g_ref[...], sh, sc).astype(BF16)
    for j in range(IN_WIDTH // A_W):
        o_ref[:, j * A_W:(j + 1) * A_W] = _dot(u, w_ref[:, j * A_W:(j + 1) * A_W]).astype(BF16)


def _inproj(ctx2d, x2d, g, csh, csc, sh, sc, w_bf):
    n_ctx, d = ctx2d.shape
    n_lat = x2d.shape[0]
    cb = n_ctx // ROW_TILE
    vec = pl.BlockSpec((1, d), lambda i: (0, 0))
    return pl.pallas_call(
        functools.partial(_inproj_kernel, n_ctx_blocks=cb),
        out_shape=jax.ShapeDtypeStruct((n_ctx + n_lat, IN_WIDTH), BF16),
        grid=((n_ctx + n_lat) // ROW_TILE,),
        in_specs=[pl.BlockSpec((ROW_TILE, d), lambda i: (jnp.minimum(i, cb - 1), 0)),
                  pl.BlockSpec((ROW_TILE, d), lambda i: (jnp.maximum(i - cb, 0), 0)),
                  vec, vec, vec, vec, vec, pl.BlockSpec((d, IN_WIDTH), lambda i: (0, 0))],
        out_specs=pl.BlockSpec((ROW_TILE, IN_WIDTH), lambda i: (i, 0)),
        compiler_params=_cparams(("arbitrary",)),
        name="inproj",
    )(ctx2d, x2d, g, csh, csc, sh, sc, w_bf)


def _scan_constants(reverse):
    c = SCAN_CHUNK
    pos = np.arange(c)[::-1] if reverse else np.arange(c)
    cum = (pos[None, :] <= pos[:, None]).astype(np.float32)
    sel = [cum]
    sgn = []
    mask = [np.eye(c, dtype=np.float32)]
    for l in range(SCAN_LEVELS):
        b = 1 << l
        pair = pos // (2 * b)
        late = (pos // b) % 2 == 1
        ref_pos = pair * 2 * b + b - 1
        sel.append((pos[None, :] <= ref_pos[:, None]).astype(np.float32))
        sgn.append(np.broadcast_to(np.where(late, 1.0, -1.0).astype(np.float32)[:, None], (c, A_DK)))
        mask.append((late[:, None] & ~late[None, :] & (pair[:, None] == pair[None, :])).astype(np.float32))
    sel.append(np.ones((c, c), np.float32))
    return (jnp.asarray(np.concatenate(sel, 0), BF16), jnp.asarray(np.stack(sgn)),
            jnp.asarray(np.stack(mask)))


def _hgrn_kernel(*refs, reverse, final):
    if final:
        (q_ref, f_ref, v_ref, lb_ref, sel_ref, sgn_ref, mask_ref, g_ref, prev_ref, ng_ref,
         o_ref, st_ref) = refs
    else:
        q_ref, f_ref, v_ref, lb_ref, sel_ref, sgn_ref, mask_ref, o_ref, st_ref = refs
    c = SCAN_CHUNK
    n_chunks = q_ref.shape[0] // c

    @pl.when(pl.program_id(0) == 0)
    def _():
        st_ref[...] = jnp.zeros_like(st_ref)

    lb = lb_ref[...]
    sel = sel_ref[...]
    q_scale = A_DK ** -0.5

    def chunk(ci, carry):
        idx = (n_chunks - 1 - ci) if reverse else ci
        rows = pl.ds(pl.multiple_of(idx * c, c), c)
        q = q_ref[rows, :].astype(F32) * q_scale
        fr = f_ref[rows, :].astype(F32)
        v = v_ref[rows, :].astype(BF16)
        f = lb + (1.0 - lb) * jax.nn.sigmoid(fr)
        k = 1.0 - f
        lf = jnp.log(f)
        hi = lf.astype(BF16)
        lo = (lf - hi.astype(F32)).astype(BF16)
        cums = _dot(sel, hi) + _dot(sel, lo)
        outs = []
        for h in range(A_HEADS):
            sl = slice(h * A_DK, (h + 1) * A_DK)
            qh, kh, vh = q[:, sl], k[:, sl], v[:, sl]
            bc = cums[0:c, sl]
            tot = cums[(SCAN_LEVELS + 1) * c:(SCAN_LEVELS + 2) * c, sl]
            scores = _dot_nt(qh.astype(BF16), kh.astype(BF16)) * mask_ref[0]
            for l in range(SCAN_LEVELS):
                beta = cums[(l + 1) * c:(l + 2) * c, sl]
                e = jnp.exp((bc - beta) * sgn_ref[l])
                scores = scores + _dot_nt((qh * e).astype(BF16), (kh * e).astype(BF16)) * mask_ref[l + 1]
            st = st_ref[h]
            o = _dot(scores.astype(BF16), vh)
            o = o + _dot_nt((qh * jnp.exp(bc)).astype(BF16), st.astype(BF16))
            kd = (kh * jnp.exp(tot - bc)).astype(BF16)
            st_ref[h] = st * jnp.exp(tot[0:1, :]) + _dot_tn(vh, kd)
            outs.append(o)
        o = jnp.concatenate(outs, axis=1)
        if final:
            o = o + prev_ref[rows, :]
            normed = []
            for h in range(A_HEADS):
                oh = o[:, h * A_DK:(h + 1) * A_DK]
                normed.append(oh * lax.rsqrt(jnp.mean(oh * oh, axis=-1, keepdims=True) + EPS))
            g = g_ref[rows, :].astype(F32)
            o = jnp.concatenate(normed, axis=1) * ng_ref[...] * (g * jax.nn.sigmoid(g))
            o_ref[rows, :] = o.astype(o_ref.dtype)
        else:
            o_ref[rows, :] = o
        return carry

    lax.fori_loop(0, n_chunks, chunk, 0)


def _hgrn_pass(proj, lb, consts, reverse, prev=None, norm_g=None):
    n = proj.shape[0]
    nb = n // ROW_TILE
    final = prev is not None
    sel, sgn, mask = consts
    if reverse:
        blk = lambda i: jnp.where(i == 0, 0, nb - i)
    else:
        blk = lambda i: i
    col = lambda cidx: pl.BlockSpec((ROW_TILE, A_W), lambda i: (blk(i), cidx))
    const = lambda a: pl.BlockSpec(a.shape, lambda i: (0,) * a.ndim)
    in_specs = [col(COL_QA), col(COL_FB if reverse else COL_FF), col(COL_IA), const(lb),
                const(sel), const(sgn), const(mask)]
    args = [proj, proj, proj, lb, sel, sgn, mask]
    if final:
        in_specs += [col(COL_GA), pl.BlockSpec((ROW_TILE, A_W), lambda i: (blk(i), 0)), const(norm_g)]
        args += [proj, prev, norm_g]
    return pl.pallas_call(
        functools.partial(_hgrn_kernel, reverse=reverse, final=final),
        out_shape=jax.ShapeDtypeStruct((n, A_W), BF16 if final else F32),
        grid=(nb,),
        in_specs=in_specs,
        out_specs=pl.BlockSpec((ROW_TILE, A_W), lambda i: (blk(i), 0)),
        scratch_shapes=[pltpu.VMEM((A_HEADS, A_DK, A_DK), F32)],
        compiler_params=_cparams(("arbitrary",)),
        name="hgrn_bwd" if reverse else "hgrn_fwd",
    )(*args)


def _natten_bias(rpb, rows):
    w = GRID_W
    n_tiles = rows // Q_ROWS
    n_dr, n_dc = 2 * WIN_ROWS - 1, 2 * WIN_COLS - 1
    cols = np.arange(w)
    wc0 = np.clip(cols - WIN_COLS // 2, 0, w - WIN_COLS)
    col_ok = (cols[None, :] >= wc0[:, None]) & (cols[None, :] < wc0[:, None] + WIN_COLS)
    dc = cols[None, :] - cols[:, None] + WIN_COLS - 1
    col_sel = (dc[:, :, None] == np.arange(n_dc)) & col_ok[:, :, None]
    row_sel, row_neg = [], []
    for tile in (0, 1, n_tiles - 1):
        r = tile * Q_ROWS + np.arange(Q_ROWS)
        start = np.clip(tile * Q_ROWS - WIN_ROWS // 2, 0, rows - K_ROWS)
        kr = start + np.arange(K_ROWS)
        wr0 = np.clip(r - WIN_ROWS // 2, 0, rows - WIN_ROWS)
        row_ok = (kr[None, :] >= wr0[:, None]) & (kr[None, :] < wr0[:, None] + WIN_ROWS)
        dr = kr[None, :] - r[:, None] + WIN_ROWS - 1
        row_sel.append((dr[:, :, None] == np.arange(n_dr)) & row_ok[:, :, None])
        row_neg.append(np.where(row_ok, 0.0, NEG_BIG))
    row_sel = jnp.asarray(np.stack(row_sel), F32)
    row_neg = jnp.asarray(np.stack(row_neg), F32)
    col_neg = jnp.asarray(np.where(col_ok, 0.0, NEG_BIG), F32)
    per_dr = jnp.einsum('hrc,qkc->hrqk', rpb.astype(F32), jnp.asarray(col_sel, F32), precision=HIGHEST)
    bias = jnp.einsum('vjmr,hrqk->vhjqmk', row_sel, per_dr, precision=HIGHEST)
    bias = bias + row_neg[:, None, :, None, :, None] + col_neg[None, None, None, :, None, :]
    return bias.reshape(3, B_HEADS, Q_ROWS * w, K_ROWS * w)


def _natten_kernel(q_ref, k0_ref, k1_ref, k2_ref, v0_ref, v1_ref, v2_ref, kc_ref, vc_ref, bias_ref, o_ref):
    kv = ((k0_ref, v0_ref), (k1_ref, v1_ref), (k2_ref, v2_ref))
    blk = k0_ref.shape[0]
    lane = lax.broadcasted_iota(jnp.int32, (1, 2 * B_DH), 1)
    scale = B_DH ** -0.5
    for pair in range(B_HEADS // 2):
        sl = slice(pair * 2 * B_DH, (pair + 1) * 2 * B_DH)
        q = q_ref[:, sl]
        res = []
        for sub in range(2):
            h = 2 * pair + sub
            own = (lane >= sub * B_DH) & (lane < (sub + 1) * B_DH)
            qm = jnp.where(own, q, jnp.zeros_like(q)) * jnp.asarray(scale, BF16)
            s = [_dot_nt(qm, kr[:, sl]) + bias_ref[0, h, :, j * blk:(j + 1) * blk]
                 for j, (kr, _) in enumerate(kv)]
            s.append(_dot_nt(qm, kc_ref[:, sl]))
            m = functools.reduce(jnp.maximum, [jnp.max(x, axis=-1, keepdims=True) for x in s])
            p = [jnp.exp(x - m) for x in s]
            den = functools.reduce(jnp.add, [jnp.sum(x, axis=-1, keepdims=True) for x in p])
            acc = _dot(p[-1].astype(BF16), vc_ref[:, sl])
            for j, (_, vr) in enumerate(kv):
                acc = acc + _dot(p[j].astype(BF16), vr[:, sl])
            res.append(acc / den)
        o_ref[:, sl] = jnp.where(lane < B_DH, res[0], res[1]).astype(o_ref.dtype)


def _natten(proj, bias, n_ctx, n_lat):
    rows = n_lat // GRID_W
    n_tiles = rows // Q_ROWS
    qblk = Q_ROWS * GRID_W
    assert n_ctx == qblk and qblk == ROW_TILE and n_tiles >= 3
    off = n_ctx // qblk
    start = lambda i: jnp.clip(i - 1, 0, n_tiles - 3) + off
    variant = lambda i: jnp.where(i == 0, 0, jnp.where(i == n_tiles - 1, 2, 1))
    kvspec = lambda cidx, j: pl.BlockSpec((qblk, B_W), lambda i: (start(i) + j, cidx))
    in_specs = ([pl.BlockSpec((qblk, B_W), lambda i: (i + off, COL_QB))]
                + [kvspec(COL_KB, j) for j in range(3)] + [kvspec(COL_VB, j) for j in range(3)]
                + [pl.BlockSpec((n_ctx, B_W), lambda i: (0, COL_KB)),
                   pl.BlockSpec((n_ctx, B_W), lambda i: (0, COL_VB)),
                   pl.BlockSpec((1, B_HEADS, qblk, K_ROWS * GRID_W), lambda i: (variant(i), 0, 0, 0))])
    return pl.pallas_call(
        _natten_kernel,
        out_shape=jax.ShapeDtypeStruct((n_lat, B_W), BF16),
        grid=(n_tiles,),
        in_specs=in_specs,
        out_specs=pl.BlockSpec((qblk, B_W), lambda i: (i, 0)),
        compiler_params=_cparams(("arbitrary",)),
        name="natten",
    )(*([proj] * 9 + [bias]))


def _route(u, wr_t, b_col):
    logits = lax.dot_general(wr_t, u, (((1,), (1,)), ((), ())), precision=HIGHEST,
                             preferred_element_type=F32)
    s = jax.nn.sigmoid(logits)
    sb = s + b_col
    srow = [s[i:i + 1] for i in range(N_EXPERTS)]
    brow = [sb[i:i + 1] for i in range(N_EXPERTS)]

    def top2sum(a, b, c, d):
        hi1, lo1, hi2, lo2 = jnp.maximum(a, b), jnp.minimum(a, b), jnp.maximum(c, d), jnp.minimum(c, d)
        return jnp.maximum(hi1, hi2) + jnp.maximum(jnp.minimum(hi1, hi2), jnp.maximum(lo1, lo2))

    gs = [top2sum(*brow[EPG * g:EPG * (g + 1)]) for g in range(N_GROUPS)]
    best, gsel = gs[0], jnp.zeros_like(gs[0], dtype=jnp.int32)
    for g in range(1, N_GROUPS):
        upd = gs[g] > best
        gsel = jnp.where(upd, g, gsel)
        best = jnp.where(upd, gs[g], best)

    def pick(rows_, j):
        out = rows_[j]
        for g in range(1, N_GROUPS):
            out = jnp.where(gsel == g, rows_[EPG * g + j], out)
        return out

    bg = [pick(brow, j) for j in range(EPG)]
    sg = [pick(srow, j) for j in range(EPG)]

    def first_argmax(vals):
        m = functools.reduce(jnp.maximum, vals)
        idx = jnp.full_like(gsel, EPG - 1)
        for j in range(EPG - 2, -1, -1):
            idx = jnp.where(vals[j] == m, j, idx)
        return idx

    i1 = first_argmax(bg)
    i2 = first_argmax([jnp.where(i1 == j, -jnp.inf, bg[j]) for j in range(EPG)])

    def take(vals, idx):
        out = vals[0]
        for j in range(1, EPG):
            out = jnp.where(idx == j, vals[j], out)
        return out

    w1, w2 = take(sg, i1), take(sg, i2)
    wsum = w1 + w2
    n1, n2 = w1 / wsum, w2 / wsum
    ia, ib = jnp.minimum(i1, i2), jnp.maximum(i1, i2)
    first_low = i1 < i2
    ga, gb = jnp.where(first_low, n1, n2), jnp.where(first_low, n2, n1)
    pair_off = jnp.where(ia == 0, 0, jnp.where(ia == 1, 3, 5))
    cls = gsel * len(PAIRS) + pair_off + ib - ia - 1
    zeros = jnp.zeros_like(ga)
    return jnp.concatenate([cls.astype(F32), ga, gb] + [zeros] * 5, axis=0)


def _outproj_kernel(a_ref, b_ref, wa_ref, wb_ref, x_ref, g1_ref, n2_ref, sh2_ref, sc2_ref, wr_ref, br_ref,
                    h_ref, r_ref):
    y = _dot(a_ref[...], wa_ref[...]) + _dot(b_ref[...], wb_ref[...])
    h = x_ref[...] + g1_ref[...] * y
    h_ref[...] = h
    r_ref[...] = _route(_rms_mod(h, n2_ref[...], sh2_ref[...], sc2_ref[...]), wr_ref[...], br_ref[...])


def _outproj(a_all, b_l, w_out_bf, x2d, g1, n2, sh2, sc2, wr_t, br_col, row_block_off):
    n, d = x2d.shape
    vec = pl.BlockSpec((1, d), lambda i: (0, 0))
    const = lambda a: pl.BlockSpec(a.shape, lambda i: (0,) * a.ndim)
    return pl.pallas_call(
        _outproj_kernel,
        out_shape=(jax.ShapeDtypeStruct((n, d), F32), jax.ShapeDtypeStruct((8, n), F32)),
        grid=(n // ROW_TILE,),
        in_specs=[pl.BlockSpec((ROW_TILE, A_W), lambda i: (i + row_block_off, 0)),
                  pl.BlockSpec((ROW_TILE, B_W), lambda i: (i, 0)),
                  pl.BlockSpec((A_W, d), lambda i: (0, 0)),
                  pl.BlockSpec((B_W, d), lambda i: (1, 0)),
                  pl.BlockSpec((ROW_TILE, d), lambda i: (i, 0)),
                  vec, vec, vec, vec, const(wr_t), const(br_col)],
        out_specs=(pl.BlockSpec((ROW_TILE, d), lambda i: (i, 0)),
                   pl.BlockSpec((8, ROW_TILE), lambda i: (0, i))),
        compiler_params=_cparams(("arbitrary",)),
        name="outproj",
    )(a_all, b_l, w_out_bf, w_out_bf, x2d, g1, n2, sh2, sc2, wr_t, br_col)


def _rows_to_cols(ga, gb):
    parts = []
    for g in (ga, gb):
        rem = g
        for _ in range(3):
            piece = rem.astype(BF16).astype(F32)
            parts.append(piece)
            rem = rem - piece
    parts += [jnp.zeros_like(ga)] * 2
    mat = jnp.concatenate(parts, axis=0).astype(BF16)
    rid = lax.broadcasted_iota(jnp.int32, (8, 128), 0)
    cid = lax.broadcasted_iota(jnp.int32, (8, 128), 1)
    sel = (((rid < 3) & (cid == 0)) | ((rid >= 3) & (rid < 6) & (cid == 1))).astype(BF16)
    cols = _dot_tn(mat, sel)
    return cols[:, 0:1], cols[:, 1:2]


def _moe_kernel(ea_ref, eb_ref, ok_ref, h_ref, n2_ref, sh2_ref, sc2_ref, wr_ref, br_ref,
                wga_ref, wua_ref, wda_ref, wgb_ref, wub_ref, wdb_ref, o_ref):
    j = pl.program_id(0)

    @pl.when(ok_ref[j] == 0)
    def _():
        o_ref[...] = jnp.zeros_like(o_ref)

    @pl.when(ok_ref[j] != 0)
    def _():
        u = _rms_mod(h_ref[...], n2_ref[...], sh2_ref[...], sc2_ref[...])
        route = _route(u, wr_ref[...], br_ref[...])
        gate_a, gate_b = _rows_to_cols(route[1:2], route[2:3])
        x = u.astype(BF16)

        def hidden(wg_ref, wu_ref, gate):
            a = _dot(x, wg_ref[0])
            return ((a * jax.nn.sigmoid(a)) * _dot(x, wu_ref[0]) * gate).astype(BF16)

        y = _dot(hidden(wga_ref, wua_ref, gate_a), wda_ref[0])
        o_ref[...] = y + _dot(hidden(wgb_ref, wub_ref, gate_b), wdb_ref[0])


def _moe(h_sorted, n2, sh2, sc2, wr_t, br_col, tile_ea, tile_eb, tile_ok, wg, wu, wd):
    p, d = h_sorted.shape
    nt = p // MOE_TILE
    wspec_in = lambda tbl: pl.BlockSpec((1, d, D_EXPERT), lambda j, ea, eb, ok: ((ea, eb)[tbl][j], 0, 0))
    wspec_out = lambda tbl: pl.BlockSpec((1, D_EXPERT, d), lambda j, ea, eb, ok: ((ea, eb)[tbl][j], 0, 0))
    const = lambda a: pl.BlockSpec(a.shape, lambda j, ea, eb, ok: (0,) * a.ndim)
    row = pl.BlockSpec((MOE_TILE, d), lambda j, ea, eb, ok: (j, 0))
    grid_spec = pltpu.PrefetchScalarGridSpec(
        num_scalar_prefetch=3,
        grid=(nt,),
        in_specs=[row, const(n2), const(sh2), const(sc2), const(wr_t), const(br_col),
                  wspec_in(0), wspec_in(0), wspec_out(0), wspec_in(1), wspec_in(1), wspec_out(1)],
        out_specs=row,
    )
    return pl.pallas_call(
        _moe_kernel,
        out_shape=jax.ShapeDtypeStruct((p, d), F32),
        grid_spec=grid_spec,
        compiler_params=_cparams(("arbitrary",)),
        name="moe",
    )(tile_ea, tile_eb, tile_ok, h_sorted, n2, sh2, sc2, wr_t, br_col, wg, wu, wd, wg, wu, wd)


def _sort_tables(route, n_tiles):
    t = route.shape[1]
    cls = route[0].astype(jnp.int32)
    onehot = (cls[:, None] == jnp.arange(N_CLASSES, dtype=jnp.int32)[None, :]).astype(jnp.int32)
    csum = jnp.cumsum(onehot, axis=0)
    counts = csum[-1]
    rank = jnp.sum((csum - 1) * onehot, axis=1)
    padded = ((counts + MOE_TILE - 1) // MOE_TILE) * MOE_TILE
    ends = jnp.cumsum(padded)
    starts = ends - padded
    dest = jnp.sum(starts[None, :] * onehot, axis=1) + rank
    total = ends[-1]
    tile_start = jnp.arange(n_tiles, dtype=jnp.int32) * MOE_TILE
    tile_ok = (tile_start < total).astype(jnp.int32)
    tile_cls = jnp.sum((ends[None, :] <= jnp.minimum(tile_start, total - 1)[:, None]).astype(jnp.int32), axis=1)
    tile_cls = jnp.clip(tile_cls, 0, N_CLASSES - 1)
    pair = jnp.asarray(np.array(PAIRS, np.int32))
    group, pidx = tile_cls // len(PAIRS), tile_cls % len(PAIRS)
    tile_ea = group * EPG + pair[pidx, 0]
    tile_eb = group * EPG + pair[pidx, 1]
    p = n_tiles * MOE_TILE
    src = jnp.zeros((p,), jnp.int32).at[dest].set(jnp.arange(t, dtype=jnp.int32))
    return dest, src, tile_ea, tile_eb, tile_ok


def _gather_kernel(idx_ref, src_ref, o_ref, sem):
    base = pl.program_id(0) * GATHER_ROWS

    def row_copy(r):
        return pltpu.make_async_copy(src_ref.at[pl.ds(idx_ref[base + r], 1)], o_ref.at[pl.ds(r, 1)], sem)

    def start(r, carry):
        row_copy(r).start()
        return carry

    def wait(r, carry):
        row_copy(r).wait()
        return carry

    lax.fori_loop(0, GATHER_ROWS, start, 0, unroll=8)
    lax.fori_loop(0, GATHER_ROWS, wait, 0, unroll=8)


def _gather_rows(src, idx):
    n = idx.shape[0]
    w = src.shape[1]
    grid_spec = pltpu.PrefetchScalarGridSpec(
        num_scalar_prefetch=1,
        grid=(n // GATHER_ROWS,),
        in_specs=[pl.BlockSpec(memory_space=pl.ANY)],
        out_specs=pl.BlockSpec((GATHER_ROWS, w), lambda i, idx_ref: (i, 0)),
        scratch_shapes=[pltpu.SemaphoreType.DMA(())],
    )
    return pl.pallas_call(
        _gather_kernel,
        out_shape=jax.ShapeDtypeStruct((n, w), src.dtype),
        grid_spec=grid_spec,
        compiler_params=_cparams(("arbitrary",)),
        name="gather_rows",
    )(idx, src)


def _moe_layer(h, route, n2, sh2, sc2, wr_t, br_col, wg, wu, wd):
    t = h.shape[0]
    n_tiles = (t + N_CLASSES * (MOE_TILE - 1)) // MOE_TILE + 1
    dest, src, tile_ea, tile_eb, tile_ok = _sort_tables(route, n_tiles)
    h_sorted = _gather_rows(h, src)
    y_sorted = _moe(h_sorted, n2, sh2, sc2, wr_t, br_col, tile_ea, tile_eb, tile_ok, wg, wu, wd)
    return _gather_rows(y_sorted, dest)


def _resid_kernel(h_ref, y_ref, g_ref, o_ref):
    o_ref[...] = h_ref[...] + g_ref[...] * y_ref[...]


def _resid(h, y, g):
    n, d = h.shape
    row = pl.BlockSpec((ROW_TILE, d), lambda i: (i, 0))
    return pl.pallas_call(
        _resid_kernel, out_shape=jax.ShapeDtypeStruct((n, d), F32), grid=(n // ROW_TILE,),
        in_specs=[row, row, pl.BlockSpec((1, d), lambda i: (0, 0))], out_specs=row,
        compiler_params=_cparams(("arbitrary",)), name="resid",
    )(h, y, g)


def _pool_kernel(prev_ref, cur_ref, next_ref, n1_ref, sh1_ref, sc1_ref, wp_ref, ps_ref, g1_ref,
                 n2_ref, sh2_ref, sc2_ref, wr_ref, br_ref, h_ref, r_ref, ext_ref, *, n_tokens):
    i = pl.program_id(0)
    tm, d = cur_ref.shape
    hal = POOL_HALO
    mod = lambda x: _rms_mod(x, n1_ref[...], sh1_ref[...], sc1_ref[...])
    h = cur_ref[...]
    u = mod(h)
    ext_ref[0:hal, :] = mod(prev_ref[...]) * (i > 0).astype(F32)
    ext_ref[hal:hal + tm, :] = u
    ext_ref[hal + tm:hal + tm + hal, :] = mod(next_ref[...]) * (i < pl.num_programs(0) - 1).astype(F32)
    t = i * tm + lax.broadcasted_iota(jnp.int32, (tm, 1), 0)
    cg = d // len(POOL_WINDOWS)
    outs = []
    for g, win in enumerate(POOL_WINDOWS):
        cs = slice(g * cg, (g + 1) * cg)
        acc = ext_ref[hal - win // 2:hal - win // 2 + tm, cs]
        for dlt in range(-win // 2 + 1, win // 2):
            acc = acc + ext_ref[hal + dlt:hal + dlt + tm, cs]
        cnt = (jnp.minimum(t + win // 2, n_tokens) - jnp.maximum(t - win // 2, 0)).astype(F32)
        z = acc / cnt - u[:, cs]
        outs.append(_dot(z.astype(BF16), wp_ref[g]))
    y = jnp.concatenate(outs, axis=1) * ps_ref[...]
    h = h + g1_ref[...] * y
    h_ref[...] = h
    r_ref[...] = _route(_rms_mod(h, n2_ref[...], sh2_ref[...], sc2_ref[...]), wr_ref[...], br_ref[...])


def _pool(h, n1, sh1, sc1, wp_bf, ps, g1, n2, sh2, sc2, wr_t, br_col):
    n, d = h.shape
    per = ROW_TILE // POOL_HALO
    vec = pl.BlockSpec((1, d), lambda i: (0, 0))
    const = lambda a: pl.BlockSpec(a.shape, lambda i: (0,) * a.ndim)
    row = pl.BlockSpec((ROW_TILE, d), lambda i: (i, 0))
    return pl.pallas_call(
        functools.partial(_pool_kernel, n_tokens=n),
        out_shape=(jax.ShapeDtypeStruct((n, d), F32), jax.ShapeDtypeStruct((8, n), F32)),
        grid=(n // ROW_TILE,),
        in_specs=[pl.BlockSpec((POOL_HALO, d), lambda i: (jnp.maximum(i * per - 1, 0), 0)),
                  row,
                  pl.BlockSpec((POOL_HALO, d), lambda i: (jnp.minimum((i + 1) * per, n // POOL_HALO - 1), 0)),
                  vec, vec, vec, const(wp_bf), vec, vec, vec, vec, vec, const(wr_t), const(br_col)],
        out_specs=(row, pl.BlockSpec((8, ROW_TILE), lambda i: (0, i))),
        scratch_shapes=[pltpu.VMEM((ROW_TILE + 2 * POOL_HALO, d), F32)],
        compiler_params=_cparams(("arbitrary",)),
        name="pool",
    )(h, h, h, n1, sh1, sc1, wp_bf, ps, g1, n2, sh2, sc2, wr_t, br_col)


def _final_kernel(h_ref, y_ref, g_ref, fg_ref, o_ref):
    h = h_ref[...] + g_ref[...] * y_ref[...]
    o_ref[...] = h * lax.rsqrt(jnp.mean(h * h, axis=-1, keepdims=True) + EPS) * fg_ref[...]


def _final(h, y, g, fg):
    n, d = h.shape
    row = pl.BlockSpec((ROW_TILE, d), lambda i: (i, 0))
    vec = pl.BlockSpec((1, d), lambda i: (0, 0))
    return pl.pallas_call(
        _final_kernel, out_shape=jax.ShapeDtypeStruct((n, d), F32), grid=(n // ROW_TILE,),
        in_specs=[row, row, vec, vec], out_specs=row,
        compiler_params=_cparams(("arbitrary",)), name="final",
    )(h, y, g, fg)


def _lower_bounds(lb_logits):
    p = jax.nn.softmax(lb_logits.astype(F32), axis=0)
    return jnp.cumsum(p, axis=0)[1:] - p[0]


def kernel(x, c, ctx, c_ctx, w_mod, b_mod, norm1_g, norm2_g, w_in, w_out, lb_logits, a_norm_g, rpb,
           w_pool, pool_scale, w_router, b_router, w_gate, w_up, w_down, final_g):
    batch, n_lat, d = x.shape
    n_ctx = ctx.shape[1]
    assert batch == 1 and w_mod.shape[0] == 2 and n_lat % ROW_TILE == 0 and n_ctx % ROW_TILE == 0
    x2d, ctx2d = x[0], ctx[0]
    row = lambda v: v.reshape(1, -1)

    cond8 = jnp.zeros((8, d), F32).at[0].set(c[0]).at[1].set(c_ctx)
    mod = _adaln(cond8, w_mod, b_mod)
    mvec = lambda layer, r, k: mod[layer, r:r + 1, k * d:(k + 1) * d]
    sh1, sc1, g1, sh2, sc2, g2 = (mvec(0, 0, k) for k in range(6))
    csh1, csc1 = mvec(0, 1, 0), mvec(0, 1, 1)

    off = n_ctx // ROW_TILE
    proj = _inproj(ctx2d, x2d, row(norm1_g[0]), csh1, csc1, sh1, sc1, w_in[0].astype(BF16))

    lbs = _lower_bounds(lb_logits)[0]
    o_fwd = _hgrn_pass(proj, row(lbs[0]), _scan_constants(False), reverse=False)
    a_all = _hgrn_pass(proj, row(lbs[1]), _scan_constants(True), reverse=True, prev=o_fwd,
                       norm_g=row(a_norm_g[0]))
    b_l = _natten(proj, _natten_bias(rpb[0], n_lat // GRID_W), n_ctx, n_lat)

    wr_t = w_router.T
    br_col = b_router.reshape(-1, 1)
    h, route = _outproj(a_all, b_l, w_out[0].astype(BF16), x2d, g1, row(norm2_g[0]), sh2, sc2,
                        wr_t, br_col, off)
    y = _moe_layer(h, route, row(norm2_g[0]), sh2, sc2, wr_t, br_col,
                   w_gate[0].astype(BF16), w_up[0].astype(BF16), w_down[0].astype(BF16))
    h = _resid(h, y, g2)

    sh1, sc1, g1, sh2, sc2, g2 = (mvec(1, 0, k) for k in range(6))
    h, route = _pool(h, row(norm1_g[1]), sh1, sc1, w_pool[0].astype(BF16), row(pool_scale[0]), g1,
                     row(norm2_g[1]), sh2, sc2, wr_t, br_col)
    y = _moe_layer(h, route, row(norm2_g[1]), sh2, sc2, wr_t, br_col,
                   w_gate[1].astype(BF16), w_up[1].astype(BF16), w_down[1].astype(BF16))
    return _final(h, y, g2, row(final_g))[None]
```

```python
import functools

import numpy as np
import jax
import jax.numpy as jnp
from jax import lax
from jax.experimental import pallas as pl
from jax.experimental.pallas import tpu as pltpu

F32 = jnp.float32
BF16 = jnp.bfloat16
HIGHEST = lax.Precision.HIGHEST

EPS = 1e-6
GRID_W = 64
A_HEADS = 4
A_DK = 128
A_W = A_HEADS * A_DK
SCAN_CHUNK = 64
SCAN_LEVELS = 6
B_HEADS = 8
B_DH = 64
B_W = B_HEADS * B_DH
WIN_ROWS = 8
WIN_COLS = 16
Q_ROWS = 4
K_ROWS = Q_ROWS + WIN_ROWS
COL_QA, COL_FF, COL_FB, COL_IA, COL_GA, COL_QB, COL_KB, COL_VB = range(8)
IN_WIDTH = 8 * A_W
POOL_WINDOWS = (2, 4, 8, 16)
POOL_HALO = 8
N_EXPERTS = 16
N_GROUPS = 4
EPG = N_EXPERTS // N_GROUPS
PAIRS = ((0, 1), (0, 2), (0, 3), (1, 2), (1, 3), (2, 3))
N_CLASSES = N_GROUPS * len(PAIRS)
D_EXPERT = 512

ROW_TILE = 256
MOE_TILE = 256
GATHER_ROWS = 256
GATHER_UNROLL = 8
NEG_BIG = -1e30
VMEM_LIMIT = 56 * 1024 * 1024


def _cparams(sem):
    return pltpu.CompilerParams(dimension_semantics=sem, vmem_limit_bytes=VMEM_LIMIT)


def _dot(a, b):
    return jnp.dot(a, b, preferred_element_type=F32)


def _dot_nt(a, b):
    return lax.dot_general(a, b, (((1,), (1,)), ((), ())), preferred_element_type=F32)


def _dot_tn(a, b):
    return lax.dot_general(a, b, (((0,), (0,)), ((), ())), preferred_element_type=F32)


def _rms_mod(x, g, shift, scale):
    ms = jnp.mean(x * x, axis=-1, keepdims=True)
    return (x * lax.rsqrt(ms + EPS) * g) * (1.0 + scale) + shift


def _mod_kernel(cond_ref, w_ref, b_ref, o_ref):
    c = cond_ref[...]
    a = c * jax.nn.sigmoid(c)
    o_ref[0] = jnp.dot(a, w_ref[0], precision=HIGHEST, preferred_element_type=F32) + b_ref[0]


def _adaln(cond8, w_mod, b_mod):
    depth, d, n6 = w_mod.shape
    bn = 1536
    return pl.pallas_call(
        _mod_kernel,
        out_shape=jax.ShapeDtypeStruct((depth, 8, n6), F32),
        grid=(depth, n6 // bn),
        in_specs=[pl.BlockSpec((8, d), lambda l, j: (0, 0)),
                  pl.BlockSpec((1, d, bn), lambda l, j: (l, 0, j)),
                  pl.BlockSpec((1, 1, bn), lambda l, j: (l, 0, j))],
        out_specs=pl.BlockSpec((1, 8, bn), lambda l, j: (l, 0, j)),
        compiler_params=_cparams(("arbitrary", "arbitrary")),
        name="adaln",
    )(cond8, w_mod, b_mod.reshape(depth, 1, n6))


def _inproj_kernel(ctx_ref, x_ref, g_ref, csh_ref, csc_ref, sh_ref, sc_ref, w_ref, o_ref, *, n_ctx_blocks):
    is_ctx = pl.program_id(0) < n_ctx_blocks
    x = jnp.where(is_ctx, ctx_ref[...], x_ref[...])
    sh = jnp.where(is_ctx, csh_ref[...], sh_ref[...])
    sc = jnp.where(is_ctx, csc_ref[...], sc_ref[...])
    u = _rms_mod(x, g_ref[...], sh, sc).astype(BF16)
    for j in range(IN_WIDTH // A_W):
        o_ref[:, j * A_W:(j + 1) * A_W] = _dot(u, w_ref[:, j * A_W:(j + 1) * A_W]).astype(BF16)


def _inproj(ctx2d, x2d, g, csh, csc, sh, sc, w_bf):
    n_ctx, d = ctx2d.shape
    n_lat = x2d.shape[0]
    cb = n_ctx // ROW_TILE
    vec = pl.BlockSpec((1, d), lambda i: (0, 0))
    return pl.pallas_call(
        functools.partial(_inproj_kernel, n_ctx_blocks=cb),
        out_shape=jax.ShapeDtypeStruct((n_ctx + n_lat, IN_WIDTH), BF16),
        grid=((n_ctx + n_lat) // ROW_TILE,),
        in_specs=[pl.BlockSpec((ROW_TILE, d), lambda i: (jnp.minimum(i, cb - 1), 0)),
                  pl.BlockSpec((ROW_TILE, d), lambda i: (jnp.maximum(i - cb, 0), 0)),
                  vec, vec, vec, vec, vec, pl.BlockSpec((d, IN_WIDTH), lambda i: (0, 0))],
        out_specs=pl.BlockSpec((ROW_TILE, IN_WIDTH), lambda i: (i, 0)),
        compiler_params=_cparams(("arbitrary",)),
        name="inproj",
    )(ctx2d, x2d, g, csh, csc, sh, sc, w_bf)


def _scan_constants(reverse):
    c = SCAN_CHUNK
    pos = np.arange(c)[::-1] if reverse else np.arange(c)
    cum = (pos[None, :] <= pos[:, None]).astype(np.float32)
    sel = [cum]
    sgn = []
    mask = [np.eye(c, dtype=np.float32)]
    for l in range(SCAN_LEVELS):
        b = 1 << l
        pair = pos // (2 * b)
        late = (pos // b) % 2 == 1
        ref_pos = pair * 2 * b + b - 1
        sel.append((pos[None, :] <= ref_pos[:, None]).astype(np.float32))
        sgn.append(np.broadcast_to(np.where(late, 1.0, -1.0).astype(np.float32)[:, None], (c, A_DK)))
        mask.append((late[:, None] & ~late[None, :] & (pair[:, None] == pair[None, :])).astype(np.float32))
    sel.append(np.ones((c, c), np.float32))
    return (jnp.asarray(np.concatenate(sel, 0), BF16), jnp.asarray(np.stack(sgn)),
            jnp.asarray(np.stack(mask)))


def _hgrn_kernel(*refs, reverse, final):
    if final:
        (q_ref, f_ref, v_ref, lb_ref, sel_ref, sgn_ref, mask_ref, g_ref, prev_ref, ng_ref,
         o_ref, st_ref) = refs
    else:
        q_ref, f_ref, v_ref, lb_ref, sel_ref, sgn_ref, mask_ref, o_ref, st_ref = refs
    c = SCAN_CHUNK
    n_chunks = q_ref.shape[0] // c

    @pl.when(pl.program_id(0) == 0)
    def _():
        st_ref[...] = jnp.zeros_like(st_ref)

    lb = lb_ref[...]
    sel = sel_ref[...]
    q_scale = A_DK ** -0.5

    for ci in range(n_chunks):
        idx = (n_chunks - 1 - ci) if reverse else ci
        rows = pl.ds(idx * c, c)
        q = q_ref[rows, :].astype(F32) * q_scale
        fr = f_ref[rows, :].astype(F32)
        v = v_ref[rows, :].astype(BF16)
        f = lb + (1.0 - lb) * jax.nn.sigmoid(fr)
        k = 1.0 - f
        lf = jnp.log(f)
        hi = lf.astype(BF16)
        lo = (lf - hi.astype(F32)).astype(BF16)
        cums = _dot(sel, hi) + _dot(sel, lo)
        outs = []
        for h in range(A_HEADS):
            sl = slice(h * A_DK, (h + 1) * A_DK)
            qh, kh, vh = q[:, sl], k[:, sl], v[:, sl]
            bc = cums[0:c, sl]
            tot = cums[(SCAN_LEVELS + 1) * c:(SCAN_LEVELS + 2) * c, sl]
            scores = _dot_nt(qh.astype(BF16), kh.astype(BF16)) * mask_ref[0]
            for l in range(SCAN_LEVELS):
                beta = cums[(l + 1) * c:(l + 2) * c, sl]
                e = jnp.exp((bc - beta) * sgn_ref[l])
                scores = scores + _dot_nt((qh * e).astype(BF16), (kh * e).astype(BF16)) * mask_ref[l + 1]
            st = st_ref[h]
            o = _dot(scores.astype(BF16), vh)
            o = o + _dot_nt((qh * jnp.exp(bc)).astype(BF16), st.astype(BF16))
            kd = (kh * jnp.exp(tot - bc)).astype(BF16)
            st_ref[h] = st * jnp.exp(tot[0:1, :]) + _dot_tn(vh, kd)
            outs.append(o)
        o = jnp.concatenate(outs, axis=1)
        if final:
            o = o + prev_ref[rows, :]
            normed = []
            for h in range(A_HEADS):
                oh = o[:, h * A_DK:(h + 1) * A_DK]
                normed.append(oh * lax.rsqrt(jnp.mean(oh * oh, axis=-1, keepdims=True) + EPS))
            g = g_ref[rows, :].astype(F32)
            o = jnp.concatenate(normed, axis=1) * ng_ref[...] * (g * jax.nn.sigmoid(g))
            o_ref[rows, :] = o.astype(o_ref.dtype)
        else:
            o_ref[rows, :] = o


def _hgrn_pass(proj, lb, consts, reverse, prev=None, norm_g=None):
    n = proj.shape[0]
    nb = n // ROW_TILE
    final = prev is not None
    sel, sgn, mask = consts
    if reverse:
        blk = lambda i: jnp.where(i == 0, 0, nb - i)
    else:
        blk = lambda i: i
    col = lambda cidx: pl.BlockSpec((ROW_TILE, A_W), lambda i: (blk(i), cidx))
    const = lambda a: pl.BlockSpec(a.shape, lambda i: (0,) * a.ndim)
    in_specs = [col(COL_QA), col(COL_FB if reverse else COL_FF), col(COL_IA), const(lb),
                const(sel), const(sgn), const(mask)]
    args = [proj, proj, proj, lb, sel, sgn, mask]
    if final:
        in_specs += [col(COL_GA), pl.BlockSpec((ROW_TILE, A_W), lambda i: (blk(i), 0)), const(norm_g)]
        args += [proj, prev, norm_g]
    return pl.pallas_call(
        functools.partial(_hgrn_kernel, reverse=reverse, final=final),
        out_shape=jax.ShapeDtypeStruct((n, A_W), BF16 if final else F32),
        grid=(nb,),
        in_specs=in_specs,
        out_specs=pl.BlockSpec((ROW_TILE, A_W), lambda i: (blk(i), 0)),
        scratch_shapes=[pltpu.VMEM((A_HEADS, A_DK, A_DK), F32)],
        compiler_params=_cparams(("arbitrary",)),
        name="hgrn_bwd" if reverse else "hgrn_fwd",
    )(*args)


def _natten_bias(rpb, rows):
    w = GRID_W
    n_tiles = rows // Q_ROWS
    n_dr, n_dc = 2 * WIN_ROWS - 1, 2 * WIN_COLS - 1
    cols = np.arange(w)
    wc0 = np.clip(cols - WIN_COLS // 2, 0, w - WIN_COLS)
    col_ok = (cols[None, :] >= wc0[:, None]) & (cols[None, :] < wc0[:, None] + WIN_COLS)
    dc = cols[None, :] - cols[:, None] + WIN_COLS - 1
    col_sel = (dc[:, :, None] == np.arange(n_dc)) & col_ok[:, :, None]
    row_sel, row_neg = [], []
    for tile in (0, 1, n_tiles - 1):
        r = tile * Q_ROWS + np.arange(Q_ROWS)
        start = np.clip(tile * Q_ROWS - WIN_ROWS // 2, 0, rows - K_ROWS)
        kr = start + np.arange(K_ROWS)
        wr0 = np.clip(r - WIN_ROWS // 2, 0, rows - WIN_ROWS)
        row_ok = (kr[None, :] >= wr0[:, None]) & (kr[None, :] < wr0[:, None] + WIN_ROWS)
        dr = kr[None, :] - r[:, None] + WIN_ROWS - 1
        row_sel.append((dr[:, :, None] == np.arange(n_dr)) & row_ok[:, :, None])
        row_neg.append(np.where(row_ok, 0.0, NEG_BIG))
    row_sel = jnp.asarray(np.stack(row_sel), F32)
    row_neg = jnp.asarray(np.stack(row_neg), F32)
    col_neg = jnp.asarray(np.where(col_ok, 0.0, NEG_BIG), F32)
    per_dr = jnp.einsum('hrc,qkc->hrqk', rpb.astype(F32), jnp.asarray(col_sel, F32), precision=HIGHEST)
    bias = jnp.einsum('vjmr,hrqk->vhjqmk', row_sel, per_dr, precision=HIGHEST)
    bias = bias + row_neg[:, None, :, None, :, None] + col_neg[None, None, None, :, None, :]
    return bias.reshape(3, B_HEADS, Q_ROWS * w, K_ROWS * w)


def _natten_kernel(q_ref, k0_ref, k1_ref, k2_ref, v0_ref, v1_ref, v2_ref, kc_ref, vc_ref, bias_ref, o_ref):
    kv = ((k0_ref, v0_ref), (k1_ref, v1_ref), (k2_ref, v2_ref))
    blk = k0_ref.shape[0]
    lane = lax.broadcasted_iota(jnp.int32, (1, 2 * B_DH), 1)
    scale = B_DH ** -0.5
    for pair in range(B_HEADS // 2):
        sl = slice(pair * 2 * B_DH, (pair + 1) * 2 * B_DH)
        q = q_ref[:, sl]
        res = []
        for sub in range(2):
            h = 2 * pair + sub
            own = (lane >= sub * B_DH) & (lane < (sub + 1) * B_DH)
            qm = jnp.where(own, q, jnp.zeros_like(q)) * jnp.asarray(scale, BF16)
            s = [_dot_nt(qm, kr[:, sl]) + bias_ref[0, h, :, j * blk:(j + 1) * blk]
                 for j, (kr, _) in enumerate(kv)]
            s.append(_dot_nt(qm, kc_ref[:, sl]))
            m = functools.reduce(jnp.maximum, [jnp.max(x, axis=-1, keepdims=True) for x in s])
            p = [jnp.exp(x - m) for x in s]
            den = functools.reduce(jnp.add, [jnp.sum(x, axis=-1, keepdims=True) for x in p])
            acc = _dot(p[-1].astype(BF16), vc_ref[:, sl])
            for j, (_, vr) in enumerate(kv):
                acc = acc + _dot(p[j].astype(BF16), vr[:, sl])
            res.append(acc / den)
        o_ref[:, sl] = jnp.where(lane < B_DH, res[0], res[1]).astype(o_ref.dtype)


def _natten(proj, bias, n_ctx, n_lat):
    rows = n_lat // GRID_W
    n_tiles = rows // Q_ROWS
    qblk = Q_ROWS * GRID_W
    assert n_ctx == qblk and qblk == ROW_TILE and n_tiles >= 3
    off = n_ctx // qblk
    start = lambda i: jnp.clip(i - 1, 0, n_tiles - 3) + off
    variant = lambda i: jnp.where(i == 0, 0, jnp.where(i == n_tiles - 1, 2, 1))
    kvspec = lambda cidx, j: pl.BlockSpec((qblk, B_W), lambda i: (start(i) + j, cidx))
    in_specs = ([pl.BlockSpec((qblk, B_W), lambda i: (i + off, COL_QB))]
                + [kvspec(COL_KB, j) for j in range(3)] + [kvspec(COL_VB, j) for j in range(3)]
                + [pl.BlockSpec((n_ctx, B_W), lambda i: (0, COL_KB)),
                   pl.BlockSpec((n_ctx, B_W), lambda i: (0, COL_VB)),
                   pl.BlockSpec((1, B_HEADS, qblk, K_ROWS * GRID_W), lambda i: (variant(i), 0, 0, 0))])
    return pl.pallas_call(
        _natten_kernel,
        out_shape=jax.ShapeDtypeStruct((n_lat, B_W), BF16),
        grid=(n_tiles,),
        in_specs=in_specs,
        out_specs=pl.BlockSpec((qblk, B_W), lambda i: (i, 0)),
        compiler_params=_cparams(("arbitrary",)),
        name="natten",
    )(*([proj] * 9 + [bias]))


def _split_bf16(w):
    hi = w.astype(BF16)
    lo = (w - hi.astype(F32)).astype(BF16)
    return jnp.concatenate([hi, lo], axis=0)


def _route(u, w_split, b_col):
    e = N_EXPERTS
    u_hi = u.astype(BF16)
    u_lo = (u - u_hi.astype(F32)).astype(BF16)
    p1 = _dot_nt(w_split, u_hi)
    logits = p1[:e] + p1[e:] + _dot_nt(w_split[:e], u_lo)
    s = jax.nn.sigmoid(logits)
    sb = s + b_col
    sub = lax.broadcasted_iota(jnp.int32, sb.shape, 0)
    big = jnp.int32(2 * e)

    def partner(x, k):
        return jnp.where((sub & k) == 0, pltpu.roll(x, e - k, 0), pltpu.roll(x, k, 0))

    def group_reduce(x, op):
        k = 1
        while k < EPG:
            x = op(x, partner(x, k))
            k *= 2
        return x

    def all_reduce(x, op):
        k = e // 2
        while k >= 1:
            x = op(x, pltpu.roll(x, k, 0))
            k //= 2
        return x

    top1 = group_reduce(sb, jnp.maximum)
    i_top1 = group_reduce(jnp.where(sb == top1, sub, big), jnp.minimum)
    rest = jnp.where(sub == i_top1, -jnp.inf, sb)
    top2 = group_reduce(rest, jnp.maximum)
    i_top2 = group_reduce(jnp.where(rest == top2, sub, big), jnp.minimum)
    score = top1 + top2
    gidx = sub >> (EPG.bit_length() - 1)
    gsel = all_reduce(jnp.where(score == all_reduce(score, jnp.maximum), gidx, big), jnp.minimum)
    chosen = gidx == gsel
    i1 = all_reduce(jnp.where(chosen, i_top1, -1), jnp.maximum) & (EPG - 1)
    i2 = all_reduce(jnp.where(chosen, i_top2, -1), jnp.maximum) & (EPG - 1)
    ia, ib = jnp.minimum(i1, i2), jnp.maximum(i1, i2)
    pair_off = jnp.where(ia == 0, 0, jnp.where(ia == 1, 3, 5))
    cls = gsel * len(PAIRS) + pair_off + ib - ia - 1
    return jnp.concatenate([cls[0:1].astype(F32), jnp.zeros((7, cls.shape[1]), F32)], axis=0)


def _outproj_kernel(a_ref, b_ref, wa_ref, wb_ref, x_ref, g1_ref, n2_ref, sh2_ref, sc2_ref, wr_ref, br_ref,
                    h_ref, r_ref):
    y = _dot(a_ref[...], wa_ref[...]) + _dot(b_ref[...], wb_ref[...])
    h = x_ref[...] + g1_ref[...] * y
    h_ref[...] = h
    r_ref[...] = _route(_rms_mod(h, n2_ref[...], sh2_ref[...], sc2_ref[...]), wr_ref[...], br_ref[...])


def _outproj(a_all, b_l, w_out_bf, x2d, g1, n2, sh2, sc2, wr_t, br_col, row_block_off):
    n, d = x2d.shape
    vec = pl.BlockSpec((1, d), lambda i: (0, 0))
    const = lambda a: pl.BlockSpec(a.shape, lambda i: (0,) * a.ndim)
    return pl.pallas_call(
        _outproj_kernel,
        out_shape=(jax.ShapeDtypeStruct((n, d), F32), jax.ShapeDtypeStruct((8, n), F32)),
        grid=(n // ROW_TILE,),
        in_specs=[pl.BlockSpec((ROW_TILE, A_W), lambda i: (i + row_block_off, 0)),
                  pl.BlockSpec((ROW_TILE, B_W), lambda i: (i, 0)),
                  pl.BlockSpec((A_W, d), lambda i: (0, 0)),
                  pl.BlockSpec((B_W, d), lambda i: (1, 0)),
                  pl.BlockSpec((ROW_TILE, d), lambda i: (i, 0)),
                  vec, vec, vec, vec, const(wr_t), const(br_col)],
        out_specs=(pl.BlockSpec((ROW_TILE, d), lambda i: (i, 0)),
                   pl.BlockSpec((8, ROW_TILE), lambda i: (0, i))),
        compiler_params=_cparams(("arbitrary",)),
        name="outproj",
    )(a_all, b_l, w_out_bf, w_out_bf, x2d, g1, n2, sh2, sc2, wr_t, br_col)


def _moe_kernel(ea_ref, eb_ref, ok_ref, h_ref, n2_ref, sh2_ref, sc2_ref, ra_ref, rb_ref,
                wga_ref, wua_ref, wda_ref, wgb_ref, wub_ref, wdb_ref, o_ref):
    j = pl.program_id(0)

    @pl.when(ok_ref[j] == 0)
    def _():
        o_ref[...] = jnp.zeros_like(o_ref)

    @pl.when(ok_ref[j] != 0)
    def _():
        u = _rms_mod(h_ref[...], n2_ref[...], sh2_ref[...], sc2_ref[...])
        sa = jax.nn.sigmoid(jnp.sum(u * ra_ref[0], axis=-1, keepdims=True))
        sb = jax.nn.sigmoid(jnp.sum(u * rb_ref[0], axis=-1, keepdims=True))
        gate_a, gate_b = sa / (sa + sb), sb / (sa + sb)
        x = u.astype(BF16)

        def hidden(wg_ref, wu_ref, gate):
            a = _dot(x, wg_ref[0])
            return ((a * jax.nn.sigmoid(a)) * _dot(x, wu_ref[0]) * gate).astype(BF16)

        y = _dot(hidden(wga_ref, wua_ref, gate_a), wda_ref[0])
        o_ref[...] = y + _dot(hidden(wgb_ref, wub_ref, gate_b), wdb_ref[0])


def _moe(h_sorted, n2, sh2, sc2, wr_rows, tile_ea, tile_eb, tile_ok, wg, wu, wd):
    p, d = h_sorted.shape
    nt = p // MOE_TILE
    pick = lambda tbl: (lambda j, ea, eb, ok: ((ea, eb)[tbl][j], 0, 0))
    wspec_in = lambda tbl: pl.BlockSpec((1, d, D_EXPERT), pick(tbl))
    wspec_out = lambda tbl: pl.BlockSpec((1, D_EXPERT, d), pick(tbl))
    rspec = lambda tbl: pl.BlockSpec((1, 1, d), pick(tbl))
    vec = pl.BlockSpec((1, d), lambda j, ea, eb, ok: (0, 0))
    row = pl.BlockSpec((MOE_TILE, d), lambda j, ea, eb, ok: (j, 0))
    grid_spec = pltpu.PrefetchScalarGridSpec(
        num_scalar_prefetch=3,
        grid=(nt,),
        in_specs=[row, vec, vec, vec, rspec(0), rspec(1),
                  wspec_in(0), wspec_in(0), wspec_out(0), wspec_in(1), wspec_in(1), wspec_out(1)],
        out_specs=row,
    )
    return pl.pallas_call(
        _moe_kernel,
        out_shape=jax.ShapeDtypeStruct((p, d), F32),
        grid_spec=grid_spec,
        compiler_params=_cparams(("arbitrary",)),
        name="moe",
    )(tile_ea, tile_eb, tile_ok, h_sorted, n2, sh2, sc2, wr_rows, wr_rows, wg, wu, wd, wg, wu, wd)


def _sort_tables(route, n_tiles):
    t = route.shape[1]
    cls = route[0].astype(jnp.int32)
    onehot = (cls[:, None] == jnp.arange(N_CLASSES, dtype=jnp.int32)[None, :]).astype(jnp.int32)
    csum = jnp.cumsum(onehot, axis=0)
    counts = csum[-1]
    rank = jnp.sum((csum - 1) * onehot, axis=1)
    padded = ((counts + MOE_TILE - 1) // MOE_TILE) * MOE_TILE
    ends = jnp.cumsum(padded)
    starts = ends - padded
    dest = jnp.sum(starts[None, :] * onehot, axis=1) + rank
    total = ends[-1]
    tile_start = jnp.arange(n_tiles, dtype=jnp.int32) * MOE_TILE
    tile_ok = (tile_start < total).astype(jnp.int32)
    tile_cls = jnp.sum((ends[None, :] <= jnp.minimum(tile_start, total - 1)[:, None]).astype(jnp.int32), axis=1)
    tile_cls = jnp.clip(tile_cls, 0, N_CLASSES - 1)
    pair = jnp.asarray(np.array(PAIRS, np.int32))
    group, pidx = tile_cls // len(PAIRS), tile_cls % len(PAIRS)
    tile_ea = group * EPG + pair[pidx, 0]
    tile_eb = group * EPG + pair[pidx, 1]
    p = n_tiles * MOE_TILE
    src = (jnp.arange(p, dtype=jnp.int32) % t).at[dest].set(jnp.arange(t, dtype=jnp.int32))
    return dest, src, tile_ea, tile_eb, tile_ok


def _gather_kernel(idx_ref, src_ref, o_ref, sem):
    base = pl.program_id(0) * GATHER_ROWS

    def row_copy(r):
        return pltpu.make_async_copy(src_ref.at[pl.ds(idx_ref[base + r], 1)], o_ref.at[pl.ds(r, 1)], sem)

    def start(g, carry):
        for k in range(GATHER_UNROLL):
            row_copy(g * GATHER_UNROLL + k).start(priority=k % 2)
        return carry

    def wait(g, carry):
        for k in range(GATHER_UNROLL):
            row_copy(g * GATHER_UNROLL + k).wait()
        return carry

    lax.fori_loop(0, GATHER_ROWS // GATHER_UNROLL, start, 0)
    lax.fori_loop(0, GATHER_ROWS // GATHER_UNROLL, wait, 0)


def _gather_rows(src, idx):
    n = idx.shape[0]
    w = src.shape[1]
    grid_spec = pltpu.PrefetchScalarGridSpec(
        num_scalar_prefetch=1,
        grid=(n // GATHER_ROWS,),
        in_specs=[pl.BlockSpec(memory_space=pl.ANY)],
        out_specs=pl.BlockSpec((GATHER_ROWS, w), lambda i, idx_ref: (i, 0)),
        scratch_shapes=[pltpu.SemaphoreType.DMA(())],
    )
    return pl.pallas_call(
        _gather_kernel,
        out_shape=jax.ShapeDtypeStruct((n, w), src.dtype),
        grid_spec=grid_spec,
        compiler_params=_cparams(("arbitrary",)),
        name="gather_rows",
    )(idx, src)


def _moe_layer(h, route, n2, sh2, sc2, wr_rows, wg, wu, wd):
    t = h.shape[0]
    n_tiles = (t + N_CLASSES * (MOE_TILE - 1)) // MOE_TILE + 1
    dest, src, tile_ea, tile_eb, tile_ok = _sort_tables(route, n_tiles)
    h_sorted = _gather_rows(h, src)
    y_sorted = _moe(h_sorted, n2, sh2, sc2, wr_rows, tile_ea, tile_eb, tile_ok, wg, wu, wd)
    return _gather_rows(y_sorted, dest)


def _resid_kernel(h_ref, y_ref, g_ref, o_ref):
    o_ref[...] = h_ref[...] + g_ref[...] * y_ref[...]


def _resid(h, y, g):
    n, d = h.shape
    row = pl.BlockSpec((ROW_TILE, d), lambda i: (i, 0))
    return pl.pallas_call(
        _resid_kernel, out_shape=jax.ShapeDtypeStruct((n, d), F32), grid=(n // ROW_TILE,),
        in_specs=[row, row, pl.BlockSpec((1, d), lambda i: (0, 0))], out_specs=row,
        compiler_params=_cparams(("arbitrary",)), name="resid",
    )(h, y, g)


def _pool_kernel(prev_ref, cur_ref, next_ref, n1_ref, sh1_ref, sc1_ref, wp_ref, ps_ref, g1_ref,
                 n2_ref, sh2_ref, sc2_ref, wr_ref, br_ref, h_ref, r_ref, ext_ref, *, n_tokens):
    i = pl.program_id(0)
    tm, d = cur_ref.shape
    hal = POOL_HALO
    mod = lambda x: _rms_mod(x, n1_ref[...], sh1_ref[...], sc1_ref[...])
    h = cur_ref[...]
    u = mod(h)
    ext_ref[0:hal, :] = mod(prev_ref[...]) * (i > 0).astype(F32)
    ext_ref[hal:hal + tm, :] = u
    ext_ref[hal + tm:hal + tm + hal, :] = mod(next_ref[...]) * (i < pl.num_programs(0) - 1).astype(F32)
    t = i * tm + lax.broadcasted_iota(jnp.int32, (tm, 1), 0)
    cg = d // len(POOL_WINDOWS)
    outs = []
    for g, win in enumerate(POOL_WINDOWS):
        cs = slice(g * cg, (g + 1) * cg)
        acc = ext_ref[hal - win // 2:hal - win // 2 + tm, cs]
        for dlt in range(-win // 2 + 1, win // 2):
            acc = acc + ext_ref[hal + dlt:hal + dlt + tm, cs]
        cnt = (jnp.minimum(t + win // 2, n_tokens) - jnp.maximum(t - win // 2, 0)).astype(F32)
        z = acc / cnt - u[:, cs]
        outs.append(_dot(z.astype(BF16), wp_ref[g]))
    y = jnp.concatenate(outs, axis=1) * ps_ref[...]
    h = h + g1_ref[...] * y
    h_ref[...] = h
    r_ref[...] = _route(_rms_mod(h, n2_ref[...], sh2_ref[...], sc2_ref[...]), wr_ref[...], br_ref[...])


def _pool(h, n1, sh1, sc1, wp_bf, ps, g1, n2, sh2, sc2, wr_t, br_col):
    n, d = h.shape
    per = ROW_TILE // POOL_HALO
    vec = pl.BlockSpec((1, d), lambda i: (0, 0))
    const = lambda a: pl.BlockSpec(a.shape, lambda i: (0,) * a.ndim)
    row = pl.BlockSpec((ROW_TILE, d), lambda i: (i, 0))
    return pl.pallas_call(
        functools.partial(_pool_kernel, n_tokens=n),
        out_shape=(jax.ShapeDtypeStruct((n, d), F32), jax.ShapeDtypeStruct((8, n), F32)),
        grid=(n // ROW_TILE,),
        in_specs=[pl.BlockSpec((POOL_HALO, d), lambda i: (jnp.maximum(i * per - 1, 0), 0)),
                  row,
                  pl.BlockSpec((POOL_HALO, d), lambda i: (jnp.minimum((i + 1) * per, n // POOL_HALO - 1), 0)),
                  vec, vec, vec, const(wp_bf), vec, vec, vec, vec, vec, const(wr_t), const(br_col)],
        out_specs=(row, pl.BlockSpec((8, ROW_TILE), lambda i: (0, i))),
        scratch_shapes=[pltpu.VMEM((ROW_TILE + 2 * POOL_HALO, d), F32)],
        compiler_params=_cparams(("arbitrary",)),
        name="pool",
    )(h, h, h, n1, sh1, sc1, wp_bf, ps, g1, n2, sh2, sc2, wr_t, br_col)


def _final_kernel(h_ref, y_ref, g_ref, fg_ref, o_ref):
    h = h_ref[...] + g_ref[...] * y_ref[...]
    o_ref[...] = h * lax.rsqrt(jnp.mean(h * h, axis=-1, keepdims=True) + EPS) * fg_ref[...]


def _final(h, y, g, fg):
    n, d = h.shape
    row = pl.BlockSpec((ROW_TILE, d), lambda i: (i, 0))
    vec = pl.BlockSpec((1, d), lambda i: (0, 0))
    return pl.pallas_call(
        _final_kernel, out_shape=jax.ShapeDtypeStruct((n, d), F32), grid=(n // ROW_TILE,),
        in_specs=[row, row, vec, vec], out_specs=row,
        compiler_params=_cparams(("arbitrary",)), name="final",
    )(h, y, g, fg)


def _lower_bounds(lb_logits):
    p = jax.nn.softmax(lb_logits.astype(F32), axis=0)
    return jnp.cumsum(p, axis=0)[1:] - p[0]


def kernel(x, c, ctx, c_ctx, w_mod, b_mod, norm1_g, norm2_g, w_in, w_out, lb_logits, a_norm_g, rpb,
           w_pool, pool_scale, w_router, b_router, w_gate, w_up, w_down, final_g):
    batch, n_lat, d = x.shape
    n_ctx = ctx.shape[1]
    assert batch == 1 and w_mod.shape[0] == 2 and n_lat % ROW_TILE == 0 and n_ctx % ROW_TILE == 0
    x2d, ctx2d = x[0], ctx[0]
    row = lambda v: v.reshape(1, -1)

    cond8 = jnp.zeros((8, d), F32).at[0].set(c[0]).at[1].set(c_ctx)
    mod = _adaln(cond8, w_mod, b_mod)
    mvec = lambda layer, r, k: mod[layer, r:r + 1, k * d:(k + 1) * d]
    sh1, sc1, g1, sh2, sc2, g2 = (mvec(0, 0, k) for k in range(6))
    csh1, csc1 = mvec(0, 1, 0), mvec(0, 1, 1)

    off = n_ctx // ROW_TILE
    proj = _inproj(ctx2d, x2d, row(norm1_g[0]), csh1, csc1, sh1, sc1, w_in[0].astype(BF16))

    lbs = _lower_bounds(lb_logits)[0]
    o_fwd = _hgrn_pass(proj, row(lbs[0]), _scan_constants(False), reverse=False)
    a_all = _hgrn_pass(proj, row(lbs[1]), _scan_constants(True), reverse=True, prev=o_fwd,
                       norm_g=row(a_norm_g[0]))
    b_l = _natten(proj, _natten_bias(rpb[0], n_lat // GRID_W), n_ctx, n_lat)

    wr_t = w_router.T.astype(F32)
    wr_split = _split_bf16(wr_t)
    wr_rows = wr_t[:, None, :]
    br_col = b_router.reshape(-1, 1)
    h, route = _outproj(a_all, b_l, w_out[0].astype(BF16), x2d, g1, row(norm2_g[0]), sh2, sc2,
                        wr_split, br_col, off)
    y = _moe_layer(h, route, row(norm2_g[0]), sh2, sc2, wr_rows,
                   w_gate[0].astype(BF16), w_up[0].astype(BF16), w_down[0].astype(BF16))
    h = _resid(h, y, g2)

    sh1, sc1, g1, sh2, sc2, g2 = (mvec(1, 0, k) for k in range(6))
    h, route = _pool(h, row(norm1_g[1]), sh1, sc1, w_pool[0].astype(BF16), row(pool_scale[0]), g1,
                     row(norm2_g[1]), sh2, sc2, wr_split, br_col)
    y = _moe_layer(h, route, row(norm2_g[1]), sh2, sc2, wr_rows,
                   w_gate[1].astype(BF16), w_up[1].astype(BF16), w_down[1].astype(BF16))
    return _final(h, y, g2, row(final_g))[None]
```
